```python
import jax, jax.numpy as jnp
from jax import lax
import numpy as np

D_MODEL = 1024
BATCH = 4
SEQ = 8192
DEPTH = 1
DEC_BATCH = 32
DEC_SEQ = 64
PAST_LEN = 1024

CHUNK = 64
HEAD_DIM = 64
D_RWKV = D_MODEL // 2
D_SB = D_MODEL // 2
H_RWKV = D_RWKV // HEAD_DIM
H_SB = D_SB // HEAD_DIM
D_DECAY_LORA = 64
D_AAA_LORA = 64
D_GATE_LORA = 160
D_RWKV_IN = 3 * D_RWKV + D_DECAY_LORA + D_AAA_LORA + D_GATE_LORA
D_IN = D_RWKV_IN + 3 * D_SB + 2 * D_MODEL
SPLIT_RWKV = [D_RWKV, 2 * D_RWKV, 3 * D_RWKV, 3 * D_RWKV + D_DECAY_LORA,
              3 * D_RWKV + D_DECAY_LORA + D_AAA_LORA]
D_FF = 4 * D_MODEL
Q_BLOCK = 128
SB_SCALE = HEAD_DIM ** -0.5
RMS_EPS = 1e-6
GN_EPS = 64e-5
L2_EPS = 1e-24

kernel_name = 'rwkv7_stickbreak_gated_streaming_encoder'


def rms_norm(x, g):
    xf = x.astype(jnp.float32)
    y = xf * lax.rsqrt(jnp.mean(xf * xf, axis=-1, keepdims=True) + RMS_EPS)
    return (y * g.astype(jnp.float32)).astype(x.dtype)


def stick_breaking(q, k, v, q_pos, k_pos):
    f32 = jnp.float32
    z = jnp.einsum('bhqd,bhkd->bhqk', q.astype(f32), k.astype(f32)) * SB_SCALE
    mask = k_pos[None, :] < q_pos[:, None]
    sp = jnp.where(mask, jax.nn.softplus(z), 0.0)
    tail = lax.cumsum(sp, axis=3, reverse=True)
    att = jnp.exp(jnp.where(mask, z - tail, -jnp.inf))
    return jnp.einsum('bhqk,bhkd->bhqd', att, v.astype(f32))


def sb_prompt(q, k, v):
    b, h, t, dh = q.shape
    nb = t // Q_BLOCK
    q_blocks = jnp.moveaxis(q.reshape(b, h, nb, Q_BLOCK, dh), 2, 0)
    k_pos = jnp.arange(t)

    def block(args):
        q_blk, i = args
        q_pos = i * Q_BLOCK + jnp.arange(Q_BLOCK)
        return stick_breaking(q_blk, k, v, q_pos, k_pos)

    out = lax.map(block, (q_blocks, jnp.arange(nb)))
    return jnp.moveaxis(out, 0, 2).reshape(b, h, t, dh)


def wkv_scan(s0, r, w, k, v, a, b):
    def step(s, inp):
        r_t, w_t, k_t, v_t, a_t, b_t = inp
        sa = jnp.einsum('bhvk,bhk->bhv', s, a_t)
        s = (s * w_t[:, :, None, :] + sa[..., None] * b_t[:, :, None, :]
             + v_t[..., None] * k_t[:, :, None, :])
        return s, jnp.einsum('bhvk,bhk->bhv', s, r_t)

    xs = tuple(jnp.moveaxis(u, 1, 0) for u in (r, w, k, v, a, b))
    s, o = lax.scan(step, s0.astype(jnp.float32), xs)
    return s, jnp.moveaxis(o, 0, 1)


def rwkv7_mix(p, shift_prev, s0, mu, w0, w2, a0, a2, g2, k_k, k_a, r_k, lnx_w, lnx_b):
    bsz, t, _ = p.shape
    f32 = jnp.float32
    p_prev = jnp.concatenate([shift_prev.astype(p.dtype), p[:, :-1]], axis=1)
    pm = p + (p_prev - p) * mu
    r, k, v, wl, al, gl = jnp.split(pm, SPLIT_RWKV, axis=-1)
    r, k, v = r.astype(f32), k.astype(f32), v.astype(f32)
    w = -jax.nn.softplus(-(w0 + jnp.tanh(wl) @ w2).astype(f32)) - 0.5
    decay = jnp.exp(-jnp.exp(w))
    a = jax.nn.sigmoid((a0 + al @ a2).astype(f32))
    g = jax.nn.sigmoid(gl) @ g2
    heads = lambda u: u.reshape(bsz, t, H_RWKV, HEAD_DIM)
    kk = heads(k * k_k)
    kk = kk * lax.rsqrt(jnp.maximum(jnp.sum(kk * kk, -1, keepdims=True), L2_EPS))
    k = k * (1.0 + (a - 1.0) * k_a)
    r_h, k_h, v_h, a_h = heads(r), heads(k), heads(v), heads(a)
    s, o = wkv_scan(s0, r_h, heads(decay), k_h, v_h, -kk, kk * a_h)
    o_mean = jnp.mean(o, -1, keepdims=True)
    o_var = jnp.mean(jnp.square(o - o_mean), -1, keepdims=True)
    o = (o - o_mean) * lax.rsqrt(o_var + GN_EPS)
    bonus = jnp.sum(r_h * k_h * r_k, -1, keepdims=True) * v_h
    o = (o.reshape(bsz, t, D_RWKV) * lnx_w + lnx_b + bonus.reshape(bsz, t, D_RWKV)) * g
    return o.astype(p.dtype), s, p[:, -1:]


def hybrid_layer(x, shift_prev, s0, k_past, v_past, g_norm1, w_in, rwkv_mu, rwkv_w0,
                 rwkv_w2, rwkv_a0, rwkv_a2, rwkv_g2, rwkv_k_k, rwkv_k_a, rwkv_r_k,
                 rwkv_lnx_w, rwkv_lnx_b, sb_q_norm_g, sb_k_norm_g, w_up_a, w_up_b,
                 w_out, g_norm2, w_ff1, w_ff2):
    bsz, t, _ = x.shape
    xn = rms_norm(x, g_norm1)
    proj = xn @ w_in
    p_rwkv = proj[..., :D_RWKV_IN]
    p_sb = proj[..., D_RWKV_IN:D_RWKV_IN + 3 * D_SB]
    p_gate = proj[..., D_RWKV_IN + 3 * D_SB:]
    y_a, s_new, shift_new = rwkv7_mix(p_rwkv, shift_prev, s0, rwkv_mu, rwkv_w0, rwkv_w2,
                                      rwkv_a0, rwkv_a2, rwkv_g2, rwkv_k_k, rwkv_k_a,
                                      rwkv_r_k, rwkv_lnx_w, rwkv_lnx_b)
    heads = lambda u: u.reshape(bsz, t, H_SB, HEAD_DIM).transpose(0, 2, 1, 3)
    q, k, v = (heads(u) for u in jnp.split(p_sb, 3, axis=-1))
    q = rms_norm(q, sb_q_norm_g)
    k = rms_norm(k, sb_k_norm_g)
    if k_past is None:
        y_b = sb_prompt(q, k, v)
    else:
        past = k_past.shape[2]
        k_all = jnp.concatenate([k_past.astype(k.dtype), k], axis=2)
        v_all = jnp.concatenate([v_past.astype(v.dtype), v], axis=2)
        y_b = stick_breaking(q, k_all, v_all, past + jnp.arange(t), jnp.arange(past + t))
    y_b = y_b.transpose(0, 2, 1, 3).reshape(bsz, t, D_SB).astype(x.dtype)
    gate = jax.nn.sigmoid(p_gate.astype(jnp.float32)).astype(x.dtype)
    merged = gate[..., :D_MODEL] * (y_a @ w_up_a) + gate[..., D_MODEL:] * (y_b @ w_up_b)
    x = x + merged @ w_out
    h = jax.nn.relu(rms_norm(x, g_norm2) @ w_ff1)
    x = x + jnp.square(h) @ w_ff2
    return x, shift_new, s_new.astype(x.dtype), k, v


def setup_inputs(seed: int = 0) -> dict:
    key = jax.random.key(seed)
    ks = jax.random.split(key, 32)
    f32 = jnp.float32
    n = lambda i, shape, scale: jax.random.normal(ks[i], shape, f32) * scale
    L = DEPTH
    return {
        'x_prompt': n(0, (BATCH, SEQ, D_MODEL), 1.0),
        'x_sample': n(1, (DEC_BATCH, DEC_SEQ, D_MODEL), 1.0),
        'cache_sb_k': n(2, (L, DEC_BATCH, H_SB, PAST_LEN, HEAD_DIM), 1.0),
        'cache_sb_v': n(3, (L, DEC_BATCH, H_SB, PAST_LEN, HEAD_DIM), 1.0),
        'state_rwkv_wkv': n(4, (L, DEC_BATCH, H_RWKV, HEAD_DIM, HEAD_DIM), 0.3),
        'state_rwkv_shift': n(5, (L, DEC_BATCH, 1, D_RWKV_IN), 1.0),
        'g_norm1': 1.0 + n(6, (L, D_MODEL), 0.02),
        'w_in': n(7, (L, D_MODEL, D_IN), D_MODEL ** -0.5),
        'rwkv_mu': jax.random.uniform(ks[8], (L, D_RWKV_IN), f32),
        'rwkv_w0': jax.random.uniform(ks[9], (L, D_RWKV), f32, -6.0, -1.0),
        'rwkv_w2': n(10, (L, D_DECAY_LORA, D_RWKV), 0.1),
        'rwkv_a0': n(11, (L, D_RWKV), 0.1),
        'rwkv_a2': n(12, (L, D_AAA_LORA, D_RWKV), 0.1),
        'rwkv_g2': n(13, (L, D_GATE_LORA, D_RWKV), D_GATE_LORA ** -0.5),
        'rwkv_k_k': 0.85 + n(14, (L, D_RWKV), 0.02),
        'rwkv_k_a': 1.0 + n(15, (L, D_RWKV), 0.02),
        'rwkv_r_k': n(16, (L, H_RWKV, HEAD_DIM), 0.1),
        'rwkv_lnx_w': 1.0 + n(17, (L, D_RWKV), 0.02),
        'rwkv_lnx_b': n(18, (L, D_RWKV), 0.02),
        'sb_q_norm_g': 1.0 + n(19, (L, HEAD_DIM), 0.02),
        'sb_k_norm_g': 1.0 + n(20, (L, HEAD_DIM), 0.02),
        'w_up_a': n(21, (L, D_RWKV, D_MODEL), D_RWKV ** -0.5),
        'w_up_b': n(22, (L, D_SB, D_MODEL), D_SB ** -0.5),
        'w_out': n(23, (L, D_MODEL, D_MODEL), D_MODEL ** -0.5),
        'g_norm2': 1.0 + n(24, (L, D_MODEL), 0.02),
        'w_ff1': n(25, (L, D_MODEL, D_FF), D_MODEL ** -0.5),
        'w_ff2': n(26, (L, D_FF, D_MODEL), D_FF ** -0.5),
    }


def reference(x_prompt, x_sample, cache_sb_k, cache_sb_v, state_rwkv_wkv, state_rwkv_shift,
              g_norm1, w_in, rwkv_mu, rwkv_w0, rwkv_w2, rwkv_a0, rwkv_a2, rwkv_g2,
              rwkv_k_k, rwkv_k_a, rwkv_r_k, rwkv_lnx_w, rwkv_lnx_b, sb_q_norm_g,
              sb_k_norm_g, w_up_a, w_up_b, w_out, g_norm2, w_ff1, w_ff2):
    x_p, x_s = x_prompt, x_sample
    kp_l, vp_l, sp_l, shp_l = [], [], [], []
    ks_l, vs_l, ss_l, shs_l = [], [], [], []
    for l in range(DEPTH):
        lw = (g_norm1[l], w_in[l], rwkv_mu[l], rwkv_w0[l], rwkv_w2[l], rwkv_a0[l],
              rwkv_a2[l], rwkv_g2[l], rwkv_k_k[l], rwkv_k_a[l], rwkv_r_k[l],
              rwkv_lnx_w[l], rwkv_lnx_b[l], sb_q_norm_g[l], sb_k_norm_g[l],
              w_up_a[l], w_up_b[l], w_out[l], g_norm2[l], w_ff1[l], w_ff2[l])
        s0_p = jnp.zeros((x_p.shape[0], H_RWKV, HEAD_DIM, HEAD_DIM), jnp.float32)
        shift0_p = jnp.zeros((x_p.shape[0], 1, D_RWKV_IN), x_p.dtype)
        x_p, sh_p, s_p, k_p, v_p = hybrid_layer(x_p, shift0_p, s0_p, None, None, *lw)
        x_s, sh_s, s_s, k_s, v_s = hybrid_layer(x_s, state_rwkv_shift[l], state_rwkv_wkv[l],
                                                cache_sb_k[l], cache_sb_v[l], *lw)
        kp_l.append(k_p); vp_l.append(v_p); sp_l.append(s_p); shp_l.append(sh_p)
        ks_l.append(k_s); vs_l.append(v_s); ss_l.append(s_s); shs_l.append(sh_s)
    return (x_p, x_s, jnp.stack(kp_l), jnp.stack(vp_l), jnp.stack(sp_l), jnp.stack(shp_l),
            jnp.stack(ks_l), jnp.stack(vs_l), jnp.stack(ss_l), jnp.stack(shs_l))
```

```python
import functools

import jax
import jax.numpy as jnp
from jax import lax
from jax.experimental import pallas as pl
from jax.experimental.pallas import tpu as pltpu

F32 = jnp.float32
BF16 = jnp.bfloat16

D_MODEL = 1024
HEAD_DIM = 64
D_RWKV = 512
D_SB = 512
N_HEADS = 8
N_PAIRS = 4
LANES = 128
D_DECAY_LORA = 64
D_AAA_LORA = 64
D_GATE_LORA = 160
D_RWKV_IN = 3 * D_RWKV + D_DECAY_LORA + D_AAA_LORA + D_GATE_LORA
D_FF = 4 * D_MODEL
CHUNK = 64
SB_SCALE = HEAD_DIM ** -0.5
RMS_EPS = 1e-6
GN_EPS = 64e-5
L2_EPS = 1e-24

PW_OFF = 3 * D_RWKV
PA_OFF = PW_OFF + LANES
PG_OFF = PA_OFF + LANES
D_RWKV_PAD = PG_OFF + 2 * LANES
SB_OFF = D_RWKV_PAD
GATE_OFF = SB_OFF + 3 * D_SB
D_IN_PAD = GATE_OFF + 2 * D_MODEL

VMEM_LIMIT = 56 * 1024 * 1024


def _dot(a, b):
    return jnp.dot(a, b, preferred_element_type=F32)


def _dot_nt(a, b):
    return lax.dot_general(a, b, (((1,), (1,)), ((), ())), preferred_element_type=F32)


def _split(x):
    hi = x.astype(BF16)
    lo = (x - hi.astype(F32)).astype(BF16)
    return hi, lo


def _dot_hl(x, m):
    hi, lo = _split(x)
    return _dot(hi, m) + _dot(lo, m)


def _dot_lh(m, x):
    hi, lo = _split(x)
    return _dot(m, hi) + _dot(m, lo)


def _mm(a, b):
    return _dot(a.astype(BF16), b.astype(BF16))


def _mm_nt(a, b):
    return _dot_nt(a.astype(BF16), b.astype(BF16))


def _softplus(z):
    return jnp.maximum(z, 0.0) + jnp.log(1.0 + jnp.exp(-jnp.abs(z)))


def _sigmoid(z):
    return 1.0 / (1.0 + jnp.exp(-z))


def _inproj_kernel(x_ref, g1_ref, w_ref, gq_ref, gk_ref, bd_ref,
                   pr_ref, q_ref, k_ref, v_ref, kb_ref, vb_ref, gate_ref):
    x = x_ref[...]
    ms = jnp.mean(x * x, axis=-1, keepdims=True)
    xn = (x * lax.rsqrt(ms + RMS_EPS) * g1_ref[...]).astype(BF16)
    pr_ref[...] = _dot(xn, w_ref[:, 0:D_RWKV_PAD])
    bd = bd_ref[...]

    def head_norm(u, g):
        ss = _dot_hl(u * u, bd) * (1.0 / HEAD_DIM)
        return u * lax.rsqrt(ss + RMS_EPS) * g

    q = head_norm(_dot(xn, w_ref[:, SB_OFF:SB_OFF + D_SB]), gq_ref[...])
    k = head_norm(_dot(xn, w_ref[:, SB_OFF + D_SB:SB_OFF + 2 * D_SB]), gk_ref[...])
    v = _dot(xn, w_ref[:, SB_OFF + 2 * D_SB:SB_OFF + 3 * D_SB])
    q_ref[...] = (q * SB_SCALE).astype(BF16)
    k_ref[...] = k
    v_ref[...] = v
    kb_ref[...] = k.astype(BF16)
    vb_ref[...] = v.astype(BF16)
    gate_ref[...] = _sigmoid(_dot(xn, w_ref[:, GATE_OFF:GATE_OFF + 2 * D_MODEL]))


def _inproj(x2d, g1, w_all, gq, gk, bd, tm):
    n = x2d.shape[0]
    row = lambda w: pl.BlockSpec((tm, w), lambda i: (i, 0))
    const = lambda a: pl.BlockSpec(a.shape, lambda i: (0, 0))
    return pl.pallas_call(
        _inproj_kernel,
        grid=(n // tm,),
        in_specs=[row(D_MODEL), const(g1), const(w_all), const(gq), const(gk), const(bd)],
        out_specs=[row(D_RWKV_PAD), row(D_SB), row(D_SB), row(D_SB), row(D_SB), row(D_SB),
                   row(2 * D_MODEL)],
        out_shape=[jax.ShapeDtypeStruct((n, D_RWKV_PAD), F32),
                   jax.ShapeDtypeStruct((n, D_SB), BF16),
                   jax.ShapeDtypeStruct((n, D_SB), F32),
                   jax.ShapeDtypeStruct((n, D_SB), F32),
                   jax.ShapeDtypeStruct((n, D_SB), BF16),
                   jax.ShapeDtypeStruct((n, D_SB), BF16),
                   jax.ShapeDtypeStruct((n, 2 * D_MODEL), F32)],
        compiler_params=pltpu.CompilerParams(dimension_semantics=("parallel",),
                                             vmem_limit_bytes=VMEM_LIMIT),
        name="inproj",
    )(x2d, g1, w_all, gq, gk, bd)


def _rwkv_kernel(p_ref, shift_ref, s0_ref, mu_ref, w0_ref, w2_ref, a0_ref, a2_ref, g2_ref,
                 kk_ref, ka_ref, rk_ref, lw_ref, lb_ref, bd_ref, tri_ref,
                 y_ref, sout_ref, carry_sc, s_sc):
    c = pl.program_id(1)

    @pl.when(c == 0)
    def _():
        carry_sc[...] = shift_ref[0]
        s_sc[...] = s0_ref[0]

    p = p_ref[0]
    row = lax.broadcasted_iota(jnp.int32, p.shape, 0)
    prev = jnp.where(row == 0, carry_sc[...], pltpu.roll(p, 1, 0))
    carry_sc[...] = p[CHUNK - 1:CHUNK, :]
    pm = p + (prev - p) * mu_ref[...]
    r = pm[:, 0:D_RWKV]
    k = pm[:, D_RWKV:2 * D_RWKV]
    v = pm[:, 2 * D_RWKV:3 * D_RWKV]
    wl = pm[:, PW_OFF:PW_OFF + LANES]
    al = pm[:, PA_OFF:PA_OFF + LANES]
    gl = pm[:, PG_OFF:PG_OFF + 2 * LANES]

    bd = bd_ref[...]
    w = -_softplus(-(w0_ref[...] + _mm(jnp.tanh(wl), w2_ref[...]))) - 0.5
    ld = -jnp.exp(w)
    a = _sigmoid(a0_ref[...] + _mm(al, a2_ref[...]))
    g = _mm(_sigmoid(gl), g2_ref[...])
    kk = k * kk_ref[...]
    kk = kk * lax.rsqrt(jnp.maximum(_dot_hl(kk * kk, bd), L2_EPS))
    k = k * (1.0 + (a - 1.0) * ka_ref[...])
    bonus = _dot_hl(r * k * rk_ref[...], bd) * v

    cl = _dot_lh(tri_ref[...], ld)
    wc = jnp.exp(cl[CHUNK - 1:CHUNK, :])
    w_inv = jnp.exp(-cl)
    at = -kk * jnp.exp(cl - ld)
    bt = kk * a * w_inv
    kt = k * w_inv
    rt = r * jnp.exp(cl)

    lane = lax.broadcasted_iota(jnp.int32, (CHUNK, LANES), 1)
    m0 = lane < HEAD_DIM
    ri = lax.broadcasted_iota(jnp.int32, (LANES, LANES), 0)
    ci = lax.broadcasted_iota(jnp.int32, (LANES, LANES), 1)
    same = (ri // CHUNK) == (ci // CHUNK)
    strict = same & ((ri % CHUNK) > (ci % CHUNK))
    incl = same & ((ri % CHUNK) >= (ci % CHUNK))
    eye = (ri == ci).astype(F32)

    def stack(u):
        return jnp.concatenate([jnp.where(m0, u, 0.0), jnp.where(m0, 0.0, u)], axis=0)

    outs = []
    for pr in range(N_PAIRS):
        sl = slice(pr * LANES, (pr + 1) * LANES)
        s_bd = s_sc[pr]
        la, lr = stack(at[:, sl]), stack(rt[:, sl])
        rb, rk = stack(bt[:, sl]), stack(kt[:, sl])
        vst = stack(v[:, sl])
        a_ab = jnp.where(strict, _mm_nt(la, rb), 0.0)
        a_ak = jnp.where(strict, _mm_nt(la, rk), 0.0)
        a_rb = jnp.where(incl, _mm_nt(lr, rb), 0.0)
        a_rk = jnp.where(incl, _mm_nt(lr, rk), 0.0)
        tinv = eye + a_ab
        apow = a_ab
        for _ in range(5):
            apow = _mm(apow, apow)
            tinv = tinv + _mm(tinv, apow)
        xs = stack(_mm_nt(at[:, sl], s_bd)) + _mm(a_ak, vst)
        u = _mm(tinv, xs)
        o_st = stack(_mm_nt(rt[:, sl], s_bd)) + _mm(a_rb, u) + _mm(a_rk, vst)
        outs.append(o_st[0:CHUNK] + o_st[CHUNK:2 * CHUNK])
        s_new = (s_bd + _mm(u.T, rb) + _mm(vst.T, rk)) * wc[:, sl]
        s_sc[pr] = s_new
        sout_ref[0, pr] = s_new
    o = jnp.concatenate(outs, axis=1)

    mean = _dot_hl(o, bd) * (1.0 / HEAD_DIM)
    oc = o - mean
    var = _dot_hl(oc * oc, bd) * (1.0 / HEAD_DIM)
    o = oc * lax.rsqrt(var + GN_EPS)
    y_ref[0] = (o * lw_ref[...] + lb_ref[...] + bonus) * g


def _rwkv(p3d, shift_pad, s0_bd, params, bd, tri):
    b, t, _ = p3d.shape
    const = lambda a: pl.BlockSpec(a.shape, lambda i, j: (0,) * a.ndim)
    return pl.pallas_call(
        _rwkv_kernel,
        grid=(b, t // CHUNK),
        in_specs=[pl.BlockSpec((1, CHUNK, D_RWKV_PAD), lambda i, j: (i, j, 0)),
                  pl.BlockSpec((1, 1, D_RWKV_PAD), lambda i, j: (i, 0, 0)),
                  pl.BlockSpec((1, N_PAIRS, LANES, LANES), lambda i, j: (i, 0, 0, 0))]
                 + [const(a) for a in params] + [const(bd), const(tri)],
        out_specs=[pl.BlockSpec((1, CHUNK, D_RWKV), lambda i, j: (i, j, 0)),
                   pl.BlockSpec((1, N_PAIRS, LANES, LANES), lambda i, j: (i, 0, 0, 0))],
        out_shape=[jax.ShapeDtypeStruct((b, t, D_RWKV), F32),
                   jax.ShapeDtypeStruct((b, N_PAIRS, LANES, LANES), F32)],
        scratch_shapes=[pltpu.VMEM((1, D_RWKV_PAD), F32),
                        pltpu.VMEM((N_PAIRS, LANES, LANES), F32)],
        compiler_params=pltpu.CompilerParams(dimension_semantics=("parallel", "arbitrary"),
                                             vmem_limit_bytes=VMEM_LIMIT),
        name="rwkv",
    )(p3d, shift_pad, s0_bd, *params, bd, tri)


def _sb_block(qm, kblk, vblk, m2, carry, acc, mask):
    z = _dot_nt(qm, kblk)
    sp = _softplus(z)
    if mask is not None:
        sp = jnp.where(mask, sp, 0.0)
    hi, lo = _split(sp)
    if m2.shape[0] == 2 * sp.shape[1]:
        tail = _dot(jnp.concatenate([hi, lo], axis=1), m2)
    else:
        tail = _dot(hi, m2) + _dot(lo, m2)
    att = jnp.exp(z - tail - carry)
    if mask is not None:
        att = jnp.where(mask, att, 0.0)
    acc = acc + _dot(att.astype(BF16), vblk)
    carry = carry + jnp.broadcast_to(tail[:, 0:1], carry.shape)
    return carry, acc


def _sb_prompt_kernel(q_ref, k_ref, v_ref, m2_ref, y_ref, *, tq, tk):
    i = pl.program_id(1)
    nd = tq // tk
    m2 = m2_ref[...]
    lane = lax.broadcasted_iota(jnp.int32, (tq, LANES), 1)
    m0 = lane < HEAD_DIM
    qpos = i * tq + lax.broadcasted_iota(jnp.int32, (tq, tk), 0)
    kcol = lax.broadcasted_iota(jnp.int32, (tq, tk), 1)
    for pr in range(N_PAIRS):
        sl = slice(pr * LANES, (pr + 1) * LANES)
        qp = q_ref[0, :, sl]
        accs = []
        for e in range(2):
            qm = jnp.where(m0 if e == 0 else jnp.logical_not(m0), qp, jnp.zeros_like(qp))

            def step(j, carry, acc, masked, qm=qm, sl=sl):
                start = pl.multiple_of(j * tk, tk)
                kblk = k_ref[0, pl.ds(start, tk), sl]
                vblk = v_ref[0, pl.ds(start, tk), sl]
                mask = (j * tk + kcol < qpos) if masked else None
                return _sb_block(qm, kblk, vblk, m2, carry, acc, mask)

            carry = jnp.zeros((tq, tk), F32)
            acc = jnp.zeros((tq, LANES), F32)
            for d in reversed(range(nd)):
                carry, acc = step(i * nd + d, carry, acc, True)
            carry, acc = lax.fori_loop(
                0, i * nd,
                lambda jj, ca: step(i * nd - 1 - jj, ca[0], ca[1], False),
                (carry, acc))
            accs.append(acc)
        y_ref[0, :, sl] = jnp.where(m0, accs[0], accs[1])


def _sb_prompt(qb, kb, vb, m2, tq, tk):
    b, t, _ = qb.shape
    full = pl.BlockSpec((1, t, D_SB), lambda i, j: (i, 0, 0))
    tile = pl.BlockSpec((1, tq, D_SB), lambda i, j: (i, j, 0))
    return pl.pallas_call(
        functools.partial(_sb_prompt_kernel, tq=tq, tk=tk),
        grid=(b, t // tq),
        in_specs=[tile, full, full, pl.BlockSpec(m2.shape, lambda i, j: (0, 0))],
        out_specs=tile,
        out_shape=jax.ShapeDtypeStruct((b, t, D_SB), F32),
        compiler_params=pltpu.CompilerParams(dimension_semantics=("parallel", "arbitrary"),
                                             vmem_limit_bytes=VMEM_LIMIT),
        name="sb_prompt",
    )(qb, kb, vb, m2)


def _sb_sample_kernel(q_ref, kn_ref, vn_ref, kc_ref, vc_ref, m2_ref, m2n_ref, y_ref, *, tk):
    t = q_ref.shape[2]
    past = kc_ref.shape[2]
    m2, m2n = m2_ref[...], m2n_ref[...]
    causal = (lax.broadcasted_iota(jnp.int32, (t, t), 1) < lax.broadcasted_iota(jnp.int32, (t, t), 0))

    def head(h, _):
        qm = q_ref[0, h]
        carry, acc = _sb_block(qm, kn_ref[0, h].astype(BF16), vn_ref[0, h].astype(BF16), m2n,
                               jnp.zeros((t, t), F32), jnp.zeros((t, HEAD_DIM), F32), causal)
        carry = jnp.broadcast_to(carry[:, 0:1], (t, tk))
        for j in reversed(range(past // tk)):
            kblk = kc_ref[0, h, j * tk:(j + 1) * tk, :].astype(BF16)
            vblk = vc_ref[0, h, j * tk:(j + 1) * tk, :].astype(BF16)
            carry, acc = _sb_block(qm, kblk, vblk, m2, carry, acc, None)
        y_ref[0, h] = acc
        return 0

    lax.fori_loop(0, N_HEADS, head, 0)


def _sb_sample(qh, knh, vnh, kc, vc, m2, m2n, tk):
    b, h, t, d = qh.shape
    past = kc.shape[2]
    new = pl.BlockSpec((1, h, t, d), lambda i: (i, 0, 0, 0))
    old = pl.BlockSpec((1, h, past, d), lambda i: (i, 0, 0, 0))
    const = lambda a: pl.BlockSpec(a.shape, lambda i: (0, 0))
    return pl.pallas_call(
        functools.partial(_sb_sample_kernel, tk=tk),
        grid=(b,),
        in_specs=[new, new, new, old, old, const(m2), const(m2n)],
        out_specs=new,
        out_shape=jax.ShapeDtypeStruct((b, h, t, d), F32),
        compiler_params=pltpu.CompilerParams(dimension_semantics=("parallel",),
                                             vmem_limit_bytes=VMEM_LIMIT),
        name="sb_sample",
    )(qh, knh, vnh, kc, vc, m2, m2n)


def _merge_ffn_kernel(x_ref, ya_ref, yb_ref, gate_ref, wua_ref, wub_ref, wo_ref, g2_ref,
                      w1_ref, w2_ref, y_ref, *, ff_chunk):
    gate = gate_ref[...]
    merged = (gate[:, 0:D_MODEL] * _dot(ya_ref[...].astype(BF16), wua_ref[...])
              + gate[:, D_MODEL:2 * D_MODEL] * _dot(yb_ref[...].astype(BF16), wub_ref[...]))
    x = x_ref[...] + _dot(merged.astype(BF16), wo_ref[...])
    ms = jnp.mean(x * x, axis=-1, keepdims=True)
    xn = (x * lax.rsqrt(ms + RMS_EPS) * g2_ref[...]).astype(BF16)
    for c in range(D_FF // ff_chunk):
        sl = slice(c * ff_chunk, (c + 1) * ff_chunk)
        h = jnp.maximum(_dot(xn, w1_ref[:, sl]), 0.0)
        x = x + _dot((h * h).astype(BF16), w2_ref[sl, :])
    y_ref[...] = x


def _merge_ffn(x2d, ya, yb, gate, wua, wub, wo, g2, w1, w2, tm, ff_chunk):
    n = x2d.shape[0]
    row = lambda w: pl.BlockSpec((tm, w), lambda i: (i, 0))
    const = lambda a: pl.BlockSpec(a.shape, lambda i: (0, 0), pipeline_mode=pl.Buffered(1))
    return pl.pallas_call(
        functools.partial(_merge_ffn_kernel, ff_chunk=ff_chunk),
        grid=(n // tm,),
        in_specs=[row(D_MODEL), row(D_RWKV), row(D_SB), row(2 * D_MODEL),
                  const(wua), const(wub), const(wo), const(g2), const(w1), const(w2)],
        out_specs=row(D_MODEL),
        out_shape=jax.ShapeDtypeStruct((n, D_MODEL), F32),
        compiler_params=pltpu.CompilerParams(dimension_semantics=("parallel",),
                                             vmem_limit_bytes=VMEM_LIMIT),
        name="merge_ffn",
    )(x2d, ya, yb, gate, wua, wub, wo, g2, w1, w2)


def _pad_cols(a, width):
    return jnp.pad(a, [(0, 0)] * (a.ndim - 1) + [(0, width - a.shape[-1])])


def _pad_rwkv_cols(a):
    o1 = 3 * D_RWKV + D_DECAY_LORA
    o2 = o1 + D_AAA_LORA
    return jnp.concatenate([a[..., :3 * D_RWKV],
                            _pad_cols(a[..., 3 * D_RWKV:o1], LANES),
                            _pad_cols(a[..., o1:o2], LANES),
                            _pad_cols(a[..., o2:], 2 * LANES)], axis=-1)


def _unpad_rwkv_cols(a):
    return jnp.concatenate([a[..., :PW_OFF + D_DECAY_LORA],
                            a[..., PA_OFF:PA_OFF + D_AAA_LORA],
                            a[..., PG_OFF:PG_OFF + D_GATE_LORA]], axis=-1)


def _state_to_pairs(s):
    b = s.shape[0]
    s = s.reshape(b, N_PAIRS, 2, HEAD_DIM, HEAD_DIM)
    z = jnp.zeros_like(s[:, :, 0])
    top = jnp.concatenate([s[:, :, 0], z], axis=-1)
    bot = jnp.concatenate([z, s[:, :, 1]], axis=-1)
    return jnp.concatenate([top, bot], axis=-2)


def _pairs_to_state(sp):
    b = sp.shape[0]
    h0 = sp[:, :, :HEAD_DIM, :HEAD_DIM]
    h1 = sp[:, :, HEAD_DIM:, HEAD_DIM:]
    return jnp.stack([h0, h1], axis=2).reshape(b, N_HEADS, HEAD_DIM, HEAD_DIM)


def _to_heads(a):
    b, t, _ = a.shape
    return a.reshape(b, t, N_HEADS, HEAD_DIM).transpose(0, 2, 1, 3)


def _tri_ge(n):
    i = jnp.arange(n)
    return (i[:, None] >= i[None, :]).astype(BF16)


def _layer(x, shift_prev, s0, k_past, v_past, wts, consts, tm, tq, tk):
    bsz, t, _ = x.shape
    n = bsz * t
    x2d = x.reshape(n, D_MODEL)
    pr, qb, k, v, kb, vb, gate = _inproj(x2d, wts["g1"], wts["w_all"], wts["gq"], wts["gk"],
                                         consts["bd"], tm)
    pr3 = pr.reshape(bsz, t, D_RWKV_PAD)
    shift_new = _unpad_rwkv_cols(pr3[:, t - 1:t, :])
    ya, s_pairs = _rwkv(pr3, _pad_rwkv_cols(shift_prev), _state_to_pairs(s0.astype(F32)),
                        wts["rwkv"], consts["bd"], consts["tri"])
    k3, v3 = k.reshape(bsz, t, D_SB), v.reshape(bsz, t, D_SB)
    k_new, v_new = _to_heads(k3), _to_heads(v3)
    if k_past is None:
        yb = _sb_prompt(qb.reshape(bsz, t, D_SB), kb.reshape(bsz, t, D_SB),
                        vb.reshape(bsz, t, D_SB), consts["m2"], tq, tk)
    else:
        ybh = _sb_sample(_to_heads(qb.reshape(bsz, t, D_SB)), k_new, v_new, k_past, v_past,
                         consts["m2"], consts["m2n"], tk)
        yb = ybh.transpose(0, 2, 1, 3).reshape(bsz, t, D_SB)
    y = _merge_ffn(x2d, ya.reshape(n, D_RWKV), yb.reshape(n, D_SB), gate, wts["wua"], wts["wub"],
                   wts["wo"], wts["g2"], wts["w1"], wts["w2"], tm, 1024)
    return y.reshape(bsz, t, D_MODEL), shift_new, _pairs_to_state(s_pairs), k_new, v_new


def _prep_weights(g_norm1, w_in, rwkv_mu, rwkv_w0, rwkv_w2, rwkv_a0, rwkv_a2, rwkv_g2, rwkv_k_k,
                  rwkv_k_a, rwkv_r_k, rwkv_lnx_w, rwkv_lnx_b, sb_q_norm_g, sb_k_norm_g, w_up_a,
                  w_up_b, w_out, g_norm2, w_ff1, w_ff2):
    row = lambda a: a.reshape(1, -1).astype(F32)
    w_all = jnp.concatenate([_pad_rwkv_cols(w_in[:, :D_RWKV_IN]), w_in[:, D_RWKV_IN:]], axis=1)
    pad_rows = lambda a, rows: jnp.pad(a, ((0, rows - a.shape[0]), (0, 0))).astype(BF16)
    rwkv = (row(_pad_rwkv_cols(rwkv_mu)), row(rwkv_w0), pad_rows(rwkv_w2, LANES), row(rwkv_a0),
            pad_rows(rwkv_a2, LANES), pad_rows(rwkv_g2, 2 * LANES), row(rwkv_k_k), row(rwkv_k_a),
            row(rwkv_r_k), row(rwkv_lnx_w), row(rwkv_lnx_b))
    return {
        "g1": row(g_norm1), "w_all": w_all.astype(BF16),
        "gq": row(jnp.tile(sb_q_norm_g, N_HEADS)), "gk": row(jnp.tile(sb_k_norm_g, N_HEADS)),
        "rwkv": rwkv,
        "wua": w_up_a.astype(BF16), "wub": w_up_b.astype(BF16), "wo": w_out.astype(BF16),
        "g2": row(g_norm2), "w1": w_ff1.astype(BF16), "w2": w_ff2.astype(BF16),
    }


def _constants(tk, t_new):
    hd = jnp.arange(D_RWKV) // HEAD_DIM
    tri = _tri_ge(tk)
    trin = _tri_ge(t_new)
    return {
        "bd": (hd[:, None] == hd[None, :]).astype(BF16),
        "tri": _tri_ge(CHUNK),
        "m2": jnp.concatenate([tri, tri], axis=0),
        "m2n": trin,
    }


def kernel(x_prompt, x_sample, cache_sb_k, cache_sb_v, state_rwkv_wkv, state_rwkv_shift, g_norm1, w_in, rwkv_mu, rwkv_w0, rwkv_w2, rwkv_a0, rwkv_a2, rwkv_g2, rwkv_k_k, rwkv_k_a, rwkv_r_k, rwkv_lnx_w, rwkv_lnx_b, sb_q_norm_g, sb_k_norm_g, w_up_a, w_up_b, w_out, g_norm2, w_ff1, w_ff2):
    depth = w_in.shape[0]
    assert depth == 1
    layer_w = (g_norm1, w_in, rwkv_mu, rwkv_w0, rwkv_w2, rwkv_a0, rwkv_a2, rwkv_g2, rwkv_k_k,
               rwkv_k_a, rwkv_r_k, rwkv_lnx_w, rwkv_lnx_b, sb_q_norm_g, sb_k_norm_g, w_up_a,
               w_up_b, w_out, g_norm2, w_ff1, w_ff2)
    wts = _prep_weights(*(a[0] for a in layer_w))
    tk = 128
    consts = _constants(tk, x_sample.shape[1])
    bp = x_prompt.shape[0]
    s0_p = jnp.zeros((bp, N_HEADS, HEAD_DIM, HEAD_DIM), F32)
    shift0_p = jnp.zeros((bp, 1, D_RWKV_IN), x_prompt.dtype)
    y_p, sh_p, s_p, k_p, v_p = _layer(x_prompt, shift0_p, s0_p, None, None, wts, consts,
                                      256, 256, tk)
    y_s, sh_s, s_s, k_s, v_s = _layer(x_sample, state_rwkv_shift[0], state_rwkv_wkv[0],
                                      cache_sb_k[0], cache_sb_v[0], wts, consts, 256, 256, tk)
    return (y_p, y_s, k_p[None], v_p[None], s_p[None], sh_p[None],
            k_s[None], v_s[None], s_s[None], sh_s[None])
```

```python
import functools

import jax
import jax.numpy as jnp
from jax import lax
from jax.experimental import pallas as pl
from jax.experimental.pallas import tpu as pltpu

F32 = jnp.float32
BF16 = jnp.bfloat16

D_MODEL = 1024
HEAD_DIM = 64
D_RWKV = 512
D_SB = 512
N_HEADS = 8
N_PAIRS = 4
LANES = 128
D_DECAY_LORA = 64
D_AAA_LORA = 64
D_GATE_LORA = 160
D_RWKV_IN = 3 * D_RWKV + D_DECAY_LORA + D_AAA_LORA + D_GATE_LORA
D_FF = 4 * D_MODEL
CHUNK = 64
RWKV_ROWS = 256
SB_SCALE = HEAD_DIM ** -0.5
RMS_EPS = 1e-6
GN_EPS = 64e-5
L2_EPS = 1e-24

PW_OFF = 3 * D_RWKV
PA_OFF = PW_OFF + LANES
PG_OFF = PA_OFF + LANES
D_RWKV_PAD = PG_OFF + 2 * LANES
SB_OFF = D_RWKV_PAD
GATE_OFF = SB_OFF + 3 * D_SB
D_IN_PAD = GATE_OFF + 2 * D_MODEL

VMEM_LIMIT = 56 * 1024 * 1024


def _dot(a, b):
    return jnp.dot(a, b, preferred_element_type=F32)


def _dot_nt(a, b):
    return lax.dot_general(a, b, (((1,), (1,)), ((), ())), preferred_element_type=F32)


def _split(x):
    hi = x.astype(BF16)
    lo = (x - hi.astype(F32)).astype(BF16)
    return hi, lo


def _dot_hl(x, m):
    hi, lo = _split(x)
    return _dot(hi, m) + _dot(lo, m)


def _dot_lh(m, x):
    hi, lo = _split(x)
    return _dot(m, hi) + _dot(m, lo)


def _head_sum(x, bd):
    xb = x.astype(BF16)
    return jnp.concatenate([_dot(xb[:, pr * LANES:(pr + 1) * LANES], bd) for pr in range(N_PAIRS)],
                           axis=1)


def _mm(a, b):
    return _dot(a.astype(BF16), b.astype(BF16))


def _mm_nt(a, b):
    return _dot_nt(a.astype(BF16), b.astype(BF16))


def _softplus(z):
    return jnp.maximum(z, 0.0) + jnp.log(1.0 + jnp.exp(-jnp.abs(z)))


def _sigmoid(z):
    return 1.0 / (1.0 + jnp.exp(-z))


def _inproj_kernel(x_ref, g1_ref, w_ref, gq_ref, gk_ref, bd_ref,
                   pr_ref, q_ref, k_ref, v_ref, kb_ref, vb_ref, gate_ref):
    x = x_ref[...]
    ms = jnp.mean(x * x, axis=-1, keepdims=True)
    xn = (x * lax.rsqrt(ms + RMS_EPS) * g1_ref[...]).astype(BF16)
    pr_ref[...] = _dot(xn, w_ref[:, 0:D_RWKV_PAD])
    bd = bd_ref[...]

    def head_norm(u, g):
        ss = _head_sum(u * u, bd) * (1.0 / HEAD_DIM)
        return u * lax.rsqrt(ss + RMS_EPS) * g

    q = head_norm(_dot(xn, w_ref[:, SB_OFF:SB_OFF + D_SB]), gq_ref[...])
    k = head_norm(_dot(xn, w_ref[:, SB_OFF + D_SB:SB_OFF + 2 * D_SB]), gk_ref[...])
    v = _dot(xn, w_ref[:, SB_OFF + 2 * D_SB:SB_OFF + 3 * D_SB])
    q_ref[...] = (q * SB_SCALE).astype(BF16)
    k_ref[...] = k
    v_ref[...] = v
    kb_ref[...] = k.astype(BF16)
    vb_ref[...] = v.astype(BF16)
    gate_ref[...] = _sigmoid(_dot(xn, w_ref[:, GATE_OFF:GATE_OFF + 2 * D_MODEL]))


def _inproj(x2d, g1, w_all, gq, gk, bd, tm):
    n = x2d.shape[0]
    row = lambda w: pl.BlockSpec((tm, w), lambda i: (i, 0))
    const = lambda a: pl.BlockSpec(a.shape, lambda i: (0, 0))
    return pl.pallas_call(
        _inproj_kernel,
        grid=(n // tm,),
        in_specs=[row(D_MODEL), const(g1), const(w_all), const(gq), const(gk), const(bd)],
        out_specs=[row(D_RWKV_PAD), row(D_SB), row(D_SB), row(D_SB), row(D_SB), row(D_SB),
                   row(2 * D_MODEL)],
        out_shape=[jax.ShapeDtypeStruct((n, D_RWKV_PAD), F32),
                   jax.ShapeDtypeStruct((n, D_SB), BF16),
                   jax.ShapeDtypeStruct((n, D_SB), F32),
                   jax.ShapeDtypeStruct((n, D_SB), F32),
                   jax.ShapeDtypeStruct((n, D_SB), BF16),
                   jax.ShapeDtypeStruct((n, D_SB), BF16),
                   jax.ShapeDtypeStruct((n, 2 * D_MODEL), F32)],
        compiler_params=pltpu.CompilerParams(dimension_semantics=("parallel",),
                                             vmem_limit_bytes=VMEM_LIMIT),
        name="inproj",
    )(x2d, g1, w_all, gq, gk, bd)


def _rwkv_kernel(p_ref, shift_ref, s0_ref, mu_ref, w0_ref, w2_ref, a0_ref, a2_ref, g2_ref,
                 kk_ref, ka_ref, rk_ref, lw_ref, lb_ref, bd_ref, tri_ref,
                 y_ref, sout_ref, carry_sc, s_sc, *, bb, tt):
    c = pl.program_id(1)
    rows = bb * tt

    @pl.when(c == 0)
    def _():
        carry_sc[...] = shift_ref[...]
        s_sc[...] = s0_ref[...]

    p = p_ref[...].reshape(rows, D_RWKV_PAD)
    row = lax.broadcasted_iota(jnp.int32, p.shape, 0)
    prev = pltpu.roll(p, 1, 0)
    for b_ in range(bb):
        prev = jnp.where(row == b_ * tt, carry_sc[b_], prev)
        carry_sc[b_] = p[(b_ + 1) * tt - 1:(b_ + 1) * tt, :]
    pm = p + (prev - p) * mu_ref[...]
    r = pm[:, 0:D_RWKV]
    k = pm[:, D_RWKV:2 * D_RWKV]
    v = pm[:, 2 * D_RWKV:3 * D_RWKV]
    wl = pm[:, PW_OFF:PW_OFF + LANES]
    al = pm[:, PA_OFF:PA_OFF + LANES]
    gl = pm[:, PG_OFF:PG_OFF + 2 * LANES]

    bd = bd_ref[...]
    w = -_softplus(-(w0_ref[...] + _mm(jnp.tanh(wl), w2_ref[...]))) - 0.5
    ld = -jnp.exp(w)
    a = _sigmoid(a0_ref[...] + _mm(al, a2_ref[...]))
    g = _mm(_sigmoid(gl), g2_ref[...])
    kk = k * kk_ref[...]
    kk = kk * lax.rsqrt(jnp.maximum(_head_sum(kk * kk, bd), L2_EPS))
    k = k * (1.0 + (a - 1.0) * ka_ref[...])
    bonus = _head_sum(r * k * rk_ref[...], bd) * v

    n_ch = tt // CHUNK
    chunks = [(b_, ch) for b_ in range(bb) for ch in range(n_ch)]
    rows_of = lambda b_, ch: slice(b_ * tt + ch * CHUNK, b_ * tt + (ch + 1) * CHUNK)
    cl = _dot_lh(tri_ref[...], ld)
    cl_end = jnp.concatenate(
        [jnp.broadcast_to(cl[rows_of(*c).stop - 1:rows_of(*c).stop, :], (CHUNK, D_RWKV))
         for c in chunks], axis=0)
    w_inv = jnp.exp(-cl)
    w_end = jnp.exp(cl_end - cl)
    at = -kk * jnp.exp(cl - ld)
    bt = kk * a * w_inv
    kt = k * w_inv
    rt = r * jnp.exp(cl)
    be = kk * a * w_end
    ke = k * w_end
    wc = jnp.exp(cl_end)

    lane = lax.broadcasted_iota(jnp.int32, (CHUNK, LANES), 1)
    m0 = lane < HEAD_DIM
    ri = lax.broadcasted_iota(jnp.int32, (LANES, LANES), 0)
    ci = lax.broadcasted_iota(jnp.int32, (LANES, LANES), 1)
    same = (ri // CHUNK) == (ci // CHUNK)
    strict = same & ((ri % CHUNK) > (ci % CHUNK))
    incl = same & ((ri % CHUNK) >= (ci % CHUNK))
    eye = (ri == ci).astype(F32)

    def stack(u):
        return jnp.concatenate([jnp.where(m0, u, 0.0), jnp.where(m0, 0.0, u)], axis=0)

    keys = [(b_, ch, pr) for (b_, ch) in chunks for pr in range(N_PAIRS)]
    blk = lambda x, key: x[rows_of(key[0], key[1]), key[2] * LANES:(key[2] + 1) * LANES]
    la = {q: stack(blk(at, q)) for q in keys}
    lr = {q: stack(blk(rt, q)) for q in keys}
    vst = {q: stack(blk(v, q)) for q in keys}
    g4 = {q: _mm_nt(jnp.concatenate([la[q], lr[q]], axis=0),
                    jnp.concatenate([stack(blk(bt, q)), stack(blk(kt, q))], axis=0)) for q in keys}
    a_ab = {q: jnp.where(strict, g4[q][:LANES, :LANES], 0.0) for q in keys}
    a_ak = {q: jnp.where(strict, g4[q][:LANES, LANES:], 0.0) for q in keys}
    a_rb = {q: jnp.where(incl, g4[q][LANES:, :LANES], 0.0) for q in keys}
    a_rk = {q: jnp.where(incl, g4[q][LANES:, LANES:], 0.0) for q in keys}
    tinv = {q: eye + a_ab[q] for q in keys}
    apow = {q: _mm(a_ab[q], a_ab[q]) for q in keys}
    for _ in range(4):
        nxt = {q: _mm(apow[q], jnp.concatenate([apow[q], tinv[q]], axis=1)) for q in keys}
        apow = {q: nxt[q][:, :LANES] for q in keys}
        tinv = {q: tinv[q] + nxt[q][:, LANES:] for q in keys}
    tinv = {q: tinv[q] + _mm(apow[q], tinv[q]) for q in keys}
    akv = {q: _mm(jnp.concatenate([a_ak[q], a_rk[q]], axis=0), vst[q]) for q in keys}
    tlav = {q: _mm(tinv[q], jnp.concatenate([la[q], akv[q][:LANES]], axis=1)) for q in keys}
    rlon = {q: _mm(a_rb[q], tlav[q]) for q in keys}
    rl = {q: lr[q] + rlon[q][:, :LANES] for q in keys}
    on = {q: rlon[q][:, LANES:] + akv[q][LANES:] for q in keys}
    mn = {q: _mm(tlav[q].T, stack(blk(be, q))) for q in keys}
    nc = {q: mn[q][LANES:] + _mm(vst[q].T, stack(blk(ke, q))) for q in keys}

    out_rows = []
    for b_ in range(bb):
        state = [s_sc[b_, pr] for pr in range(N_PAIRS)]
        for ch in range(n_ch):
            outs = []
            for pr in range(N_PAIRS):
                q = (b_, ch, pr)
                s_b = state[pr].astype(BF16)
                o_st = _dot_nt(rl[q].astype(BF16), s_b) + on[q]
                outs.append(o_st[0:CHUNK] + o_st[CHUNK:2 * CHUNK])
                state[pr] = (state[pr] * blk(wc, q)[0:1, :] + _dot(s_b, mn[q][:LANES].astype(BF16))
                             + nc[q])
            out_rows.append(jnp.concatenate(outs, axis=1))
        for pr in range(N_PAIRS):
            s_sc[b_, pr] = state[pr]
            sout_ref[b_, pr] = state[pr]
    o = jnp.concatenate(out_rows, axis=0)

    mean = _head_sum(o, bd) * (1.0 / HEAD_DIM)
    oc = o - mean
    var = _head_sum(oc * oc, bd) * (1.0 / HEAD_DIM)
    o = oc * lax.rsqrt(var + GN_EPS)
    y_ref[...] = ((o * lw_ref[...] + lb_ref[...] + bonus) * g).reshape(bb, tt, D_RWKV)


def _rwkv(p3d, shift_pad, s0_bd, params, bd, tri, bb, tt):
    b, t, _ = p3d.shape
    const = lambda a: pl.BlockSpec(a.shape, lambda i, j: (0,) * a.ndim)
    state = pl.BlockSpec((bb, N_PAIRS, LANES, LANES), lambda i, j: (i, 0, 0, 0))
    return pl.pallas_call(
        functools.partial(_rwkv_kernel, bb=bb, tt=tt),
        grid=(b // bb, t // tt),
        in_specs=[pl.BlockSpec((bb, tt, D_RWKV_PAD), lambda i, j: (i, j, 0)),
                  pl.BlockSpec((bb, 1, D_RWKV_PAD), lambda i, j: (i, 0, 0)), state]
                 + [const(a) for a in params] + [const(bd), const(tri)],
        out_specs=[pl.BlockSpec((bb, tt, D_RWKV), lambda i, j: (i, j, 0)), state],
        out_shape=[jax.ShapeDtypeStruct((b, t, D_RWKV), F32),
                   jax.ShapeDtypeStruct((b, N_PAIRS, LANES, LANES), F32)],
        scratch_shapes=[pltpu.VMEM((bb, 1, D_RWKV_PAD), F32),
                        pltpu.VMEM((bb, N_PAIRS, LANES, LANES), F32)],
        compiler_params=pltpu.CompilerParams(dimension_semantics=("parallel", "arbitrary"),
                                             vmem_limit_bytes=VMEM_LIMIT),
        name="rwkv",
    )(p3d, shift_pad, s0_bd, *params, bd, tri)


def _sb_scores(qm, kblk, m2, carry, mask):
    z = _dot_nt(qm, kblk)
    sp = _softplus(z)
    if mask is not None:
        sp = jnp.where(mask, sp, 0.0)
    hi, lo = _split(sp)
    if m2.shape[0] == 2 * sp.shape[1]:
        tail = _dot(jnp.concatenate([hi, lo], axis=1), m2)
    else:
        tail = _dot(hi, m2) + _dot(lo, m2)
    tail = tail + carry
    att = jnp.exp(z - tail)
    if mask is not None:
        att = jnp.where(mask, att, 0.0)
    return att.astype(BF16), jnp.broadcast_to(tail[:, 0:1], carry.shape)


def _sb_prompt_kernel(q_ref, k_ref, v_ref, m2_ref, y_ref, qm_sc, acc_sc, carry_sc, *, tq, tk):
    i = pl.program_id(1)
    nd = tq // tk
    lane_q = lax.broadcasted_iota(jnp.int32, (tq, LANES), 1) < HEAD_DIM
    lane_k = lax.broadcasted_iota(jnp.int32, (tk, LANES), 1) < HEAD_DIM
    qpos = i * tq + lax.broadcasted_iota(jnp.int32, (tq, tk), 0)
    kcol = lax.broadcasted_iota(jnp.int32, (tq, tk), 1)
    for pr in range(N_PAIRS):
        qp = q_ref[0, :, pr * LANES:(pr + 1) * LANES]
        qm_sc[2 * pr] = jnp.where(lane_q, qp, jnp.zeros_like(qp))
        qm_sc[2 * pr + 1] = jnp.where(lane_q, jnp.zeros_like(qp), qp)
    acc_sc[...] = jnp.zeros_like(acc_sc)
    carry_sc[...] = jnp.zeros_like(carry_sc)

    def block(j, masked):
        start = pl.multiple_of(j * tk, tk)
        mask = (j * tk + kcol < qpos) if masked else None
        m2 = m2_ref[...]
        heads = range(N_HEADS)
        z = [_dot_nt(qm_sc[h], k_ref[0, pl.ds(start, tk), (h // 2) * LANES:(h // 2 + 1) * LANES])
             for h in heads]
        sp = [_softplus(z[h]) for h in heads]
        if masked:
            sp = [jnp.where(mask, s, 0.0) for s in sp]
        hl = [jnp.concatenate(_split(s), axis=1) for s in sp]
        tail = [_dot(hl[h], m2) + carry_sc[h] for h in heads]
        att = [jnp.exp(z[h] - tail[h]) for h in heads]
        if masked:
            att = [jnp.where(mask, a, 0.0) for a in att]
        for h in heads:
            carry_sc[h] = jnp.broadcast_to(tail[h][:, 0:1], (tq, tk))
        for pr in range(N_PAIRS):
            vblk = v_ref[0, pl.ds(start, tk), pr * LANES:(pr + 1) * LANES]
            acc_sc[pr] += (_dot(att[2 * pr].astype(BF16),
                                jnp.where(lane_k, vblk, jnp.zeros_like(vblk)))
                           + _dot(att[2 * pr + 1].astype(BF16),
                                  jnp.where(lane_k, jnp.zeros_like(vblk), vblk)))

    for d in reversed(range(nd)):
        block(i * nd + d, True)

    def body(jj, _):
        block(i * nd - 1 - jj, False)
        return 0

    lax.fori_loop(0, i * nd, body, 0)
    for pr in range(N_PAIRS):
        y_ref[0, :, pr * LANES:(pr + 1) * LANES] = acc_sc[pr]


def _sb_prompt(qb, kb, vb, m2, tq, tk):
    b, t, _ = qb.shape
    full = pl.BlockSpec((1, t, D_SB), lambda i, j: (i, 0, 0))
    tile = pl.BlockSpec((1, tq, D_SB), lambda i, j: (i, j, 0))
    return pl.pallas_call(
        functools.partial(_sb_prompt_kernel, tq=tq, tk=tk),
        grid=(b, t // tq),
        in_specs=[tile, full, full, pl.BlockSpec(m2.shape, lambda i, j: (0, 0))],
        out_specs=tile,
        out_shape=jax.ShapeDtypeStruct((b, t, D_SB), F32),
        scratch_shapes=[pltpu.VMEM((N_HEADS, tq, LANES), BF16),
                        pltpu.VMEM((N_PAIRS, tq, LANES), F32),
                        pltpu.VMEM((N_HEADS, tq, tk), F32)],
        compiler_params=pltpu.CompilerParams(dimension_semantics=("parallel", "arbitrary"),
                                             vmem_limit_bytes=VMEM_LIMIT),
        name="sb_prompt",
    )(qb, kb, vb, m2)


def _sb_sample_kernel(q_ref, kn_ref, vn_ref, kc_ref, vc_ref, m2_ref, m2n_ref, y_ref, *, tk):
    t = q_ref.shape[2]
    past = kc_ref.shape[2]
    m2, m2n = m2_ref[...], m2n_ref[...]
    causal = (lax.broadcasted_iota(jnp.int32, (t, t), 1) < lax.broadcasted_iota(jnp.int32, (t, t), 0))
    heads = range(N_HEADS)
    nb = past // tk
    zn = [_dot_nt(q_ref[0, h], kn_ref[0, h].astype(BF16)) for h in heads]
    zp = [_dot_nt(q_ref[0, h], kc_ref[0, h].astype(BF16)) for h in heads]
    spn = [jnp.where(causal, _softplus(z), 0.0) for z in zn]
    spp = [_softplus(z) for z in zp]
    tailn = [_dot_hl(s, m2n) for s in spn]
    tailp = [[_dot(jnp.concatenate(_split(s[:, c * tk:(c + 1) * tk]), axis=1), m2)
              for c in range(nb)] for s in spp]
    for h in heads:
        off = jnp.broadcast_to(tailn[h][:, 0:1], (t, tk))
        for c in reversed(range(nb)):
            tailp[h][c] = tailp[h][c] + off
            off = jnp.broadcast_to(tailp[h][c][:, 0:1], (t, tk))
    for h in heads:
        attn = jnp.where(causal, jnp.exp(zn[h] - tailn[h]), 0.0).astype(BF16)
        attp = jnp.exp(zp[h] - jnp.concatenate(tailp[h], axis=1)).astype(BF16)
        y_ref[0, h] = _dot(attn, vn_ref[0, h].astype(BF16)) + _dot(attp, vc_ref[0, h].astype(BF16))


def _sb_sample(qh, knh, vnh, kc, vc, m2, m2n, tk):
    b, h, t, d = qh.shape
    past = kc.shape[2]
    new = pl.BlockSpec((1, h, t, d), lambda i: (i, 0, 0, 0))
    old = pl.BlockSpec((1, h, past, d), lambda i: (i, 0, 0, 0))
    const = lambda a: pl.BlockSpec(a.shape, lambda i: (0, 0))
    return pl.pallas_call(
        functools.partial(_sb_sample_kernel, tk=tk),
        grid=(b,),
        in_specs=[new, new, new, old, old, const(m2), const(m2n)],
        out_specs=new,
        out_shape=jax.ShapeDtypeStruct((b, h, t, d), F32),
        compiler_params=pltpu.CompilerParams(dimension_semantics=("parallel",),
                                             vmem_limit_bytes=VMEM_LIMIT),
        name="sb_sample",
    )(qh, knh, vnh, kc, vc, m2, m2n)


def _merge_ffn_kernel(x_ref, ya_ref, yb_ref, gate_ref, wua_ref, wub_ref, wo_ref, g2_ref,
                      w1_ref, w2_ref, y_ref, *, ff_chunk):
    gate = gate_ref[...]
    merged = (gate[:, 0:D_MODEL] * _dot(ya_ref[...].astype(BF16), wua_ref[...])
              + gate[:, D_MODEL:2 * D_MODEL] * _dot(yb_ref[...].astype(BF16), wub_ref[...]))
    x = x_ref[...] + _dot(merged.astype(BF16), wo_ref[...])
    ms = jnp.mean(x * x, axis=-1, keepdims=True)
    xn = (x * lax.rsqrt(ms + RMS_EPS) * g2_ref[...]).astype(BF16)
    for c in range(D_FF // ff_chunk):
        sl = slice(c * ff_chunk, (c + 1) * ff_chunk)
        h = jnp.maximum(_dot(xn, w1_ref[:, sl]), 0.0)
        x = x + _dot((h * h).astype(BF16), w2_ref[sl, :])
    y_ref[...] = x


def _merge_ffn(x2d, ya, yb, gate, wua, wub, wo, g2, w1, w2, tm, ff_chunk):
    n = x2d.shape[0]
    row = lambda w: pl.BlockSpec((tm, w), lambda i: (i, 0))
    const = lambda a: pl.BlockSpec(a.shape, lambda i: (0, 0), pipeline_mode=pl.Buffered(1))
    return pl.pallas_call(
        functools.partial(_merge_ffn_kernel, ff_chunk=ff_chunk),
        grid=(n // tm,),
        in_specs=[row(D_MODEL), row(D_RWKV), row(D_SB), row(2 * D_MODEL),
                  const(wua), const(wub), const(wo), const(g2), const(w1), const(w2)],
        out_specs=row(D_MODEL),
        out_shape=jax.ShapeDtypeStruct((n, D_MODEL), F32),
        compiler_params=pltpu.CompilerParams(dimension_semantics=("parallel",),
                                             vmem_limit_bytes=VMEM_LIMIT),
        name="merge_ffn",
    )(x2d, ya, yb, gate, wua, wub, wo, g2, w1, w2)


def _pad_cols(a, width):
    return jnp.pad(a, [(0, 0)] * (a.ndim - 1) + [(0, width - a.shape[-1])])


def _pad_rwkv_cols(a):
    o1 = 3 * D_RWKV + D_DECAY_LORA
    o2 = o1 + D_AAA_LORA
    return jnp.concatenate([a[..., :3 * D_RWKV],
                            _pad_cols(a[..., 3 * D_RWKV:o1], LANES),
                            _pad_cols(a[..., o1:o2], LANES),
                            _pad_cols(a[..., o2:], 2 * LANES)], axis=-1)


def _unpad_rwkv_cols(a):
    return jnp.concatenate([a[..., :PW_OFF + D_DECAY_LORA],
                            a[..., PA_OFF:PA_OFF + D_AAA_LORA],
                            a[..., PG_OFF:PG_OFF + D_GATE_LORA]], axis=-1)


def _state_to_pairs(s):
    b = s.shape[0]
    s = s.reshape(b, N_PAIRS, 2, HEAD_DIM, HEAD_DIM)
    z = jnp.zeros_like(s[:, :, 0])
    top = jnp.concatenate([s[:, :, 0], z], axis=-1)
    bot = jnp.concatenate([z, s[:, :, 1]], axis=-1)
    return jnp.concatenate([top, bot], axis=-2)


def _pairs_to_state(sp):
    b = sp.shape[0]
    h0 = sp[:, :, :HEAD_DIM, :HEAD_DIM]
    h1 = sp[:, :, HEAD_DIM:, HEAD_DIM:]
    return jnp.stack([h0, h1], axis=2).reshape(b, N_HEADS, HEAD_DIM, HEAD_DIM)


def _to_heads(a):
    b, t, _ = a.shape
    return a.reshape(b, t, N_HEADS, HEAD_DIM).transpose(0, 2, 1, 3)


def _tri_ge(n):
    i = jnp.arange(n)
    return (i[:, None] >= i[None, :]).astype(BF16)


def _layer(x, shift_prev, s0, k_past, v_past, wts, consts, tm, tq, tk):
    bsz, t, _ = x.shape
    tt = min(t, RWKV_ROWS)
    bb = RWKV_ROWS // tt
    n = bsz * t
    x2d = x.reshape(n, D_MODEL)
    pr, qb, k, v, kb, vb, gate = _inproj(x2d, wts["g1"], wts["w_all"], wts["gq"], wts["gk"],
                                         consts["bd"], tm)
    pr3 = pr.reshape(bsz, t, D_RWKV_PAD)
    shift_new = _unpad_rwkv_cols(pr3[:, t - 1:t, :])
    ya, s_pairs = _rwkv(pr3, _pad_rwkv_cols(shift_prev), _state_to_pairs(s0.astype(F32)),
                        wts["rwkv"], consts["bd"], consts["tri"], bb, tt)
    k3, v3 = k.reshape(bsz, t, D_SB), v.reshape(bsz, t, D_SB)
    k_new, v_new = _to_heads(k3), _to_heads(v3)
    if k_past is None:
        yb = _sb_prompt(qb.reshape(bsz, t, D_SB), kb.reshape(bsz, t, D_SB),
                        vb.reshape(bsz, t, D_SB), consts["m2"], tq, tk)
    else:
        ybh = _sb_sample(_to_heads(qb.reshape(bsz, t, D_SB)), k_new, v_new, k_past, v_past,
                         consts["m2"], consts["m2n"], tk)
        yb = ybh.transpose(0, 2, 1, 3).reshape(bsz, t, D_SB)
    y = _merge_ffn(x2d, ya.reshape(n, D_RWKV), yb.reshape(n, D_SB), gate, wts["wua"], wts["wub"],
                   wts["wo"], wts["g2"], wts["w1"], wts["w2"], tm, 1024)
    return y.reshape(bsz, t, D_MODEL), shift_new, _pairs_to_state(s_pairs), k_new, v_new


def _prep_weights(g_norm1, w_in, rwkv_mu, rwkv_w0, rwkv_w2, rwkv_a0, rwkv_a2, rwkv_g2, rwkv_k_k,
                  rwkv_k_a, rwkv_r_k, rwkv_lnx_w, rwkv_lnx_b, sb_q_norm_g, sb_k_norm_g, w_up_a,
                  w_up_b, w_out, g_norm2, w_ff1, w_ff2):
    row = lambda a: a.reshape(1, -1).astype(F32)
    w_all = jnp.concatenate([_pad_rwkv_cols(w_in[:, :D_RWKV_IN]), w_in[:, D_RWKV_IN:]], axis=1)
    pad_rows = lambda a, rows: jnp.pad(a, ((0, rows - a.shape[0]), (0, 0))).astype(BF16)
    rwkv = (row(_pad_rwkv_cols(rwkv_mu)), row(rwkv_w0), pad_rows(rwkv_w2, LANES), row(rwkv_a0),
            pad_rows(rwkv_a2, LANES), pad_rows(rwkv_g2, 2 * LANES), row(rwkv_k_k), row(rwkv_k_a),
            row(rwkv_r_k), row(rwkv_lnx_w), row(rwkv_lnx_b))
    return {
        "g1": row(g_norm1), "w_all": w_all.astype(BF16),
        "gq": row(jnp.tile(sb_q_norm_g, N_HEADS)), "gk": row(jnp.tile(sb_k_norm_g, N_HEADS)),
        "rwkv": rwkv,
        "wua": w_up_a.astype(BF16), "wub": w_up_b.astype(BF16), "wo": w_out.astype(BF16),
        "g2": row(g_norm2), "w1": w_ff1.astype(BF16), "w2": w_ff2.astype(BF16),
    }


def _constants(tk, t_new, rwkv_rows):
    hd = jnp.arange(LANES) // HEAD_DIM
    tri = _tri_ge(tk)
    trin = _tri_ge(t_new)
    ch = jnp.arange(rwkv_rows) // CHUNK
    return {
        "bd": (hd[:, None] == hd[None, :]).astype(BF16),
        "tri": _tri_ge(rwkv_rows) * (ch[:, None] == ch[None, :]).astype(BF16),
        "m2": jnp.concatenate([tri, tri], axis=0),
        "m2n": trin,
    }


def kernel(x_prompt, x_sample, cache_sb_k, cache_sb_v, state_rwkv_wkv, state_rwkv_shift, g_norm1, w_in, rwkv_mu, rwkv_w0, rwkv_w2, rwkv_a0, rwkv_a2, rwkv_g2, rwkv_k_k, rwkv_k_a, rwkv_r_k, rwkv_lnx_w, rwkv_lnx_b, sb_q_norm_g, sb_k_norm_g, w_up_a, w_up_b, w_out, g_norm2, w_ff1, w_ff2):
    depth = w_in.shape[0]
    assert depth == 1
    layer_w = (g_norm1, w_in, rwkv_mu, rwkv_w0, rwkv_w2, rwkv_a0, rwkv_a2, rwkv_g2, rwkv_k_k,
               rwkv_k_a, rwkv_r_k, rwkv_lnx_w, rwkv_lnx_b, sb_q_norm_g, sb_k_norm_g, w_up_a,
               w_up_b, w_out, g_norm2, w_ff1, w_ff2)
    wts = _prep_weights(*(a[0] for a in layer_w))
    tk = 128
    consts = _constants(tk, x_sample.shape[1], RWKV_ROWS)
    bp = x_prompt.shape[0]
    s0_p = jnp.zeros((bp, N_HEADS, HEAD_DIM, HEAD_DIM), F32)
    shift0_p = jnp.zeros((bp, 1, D_RWKV_IN), x_prompt.dtype)
    y_p, sh_p, s_p, k_p, v_p = _layer(x_prompt, shift0_p, s0_p, None, None, wts, consts,
                                      256, 256, tk)
    y_s, sh_s, s_s, k_s, v_s = _layer(x_sample, state_rwkv_shift[0], state_rwkv_wkv[0],
                                      cache_sb_k[0], cache_sb_v[0], wts, consts, 256, 256, tk)
    return (y_p, y_s, k_p[None], v_p[None], s_p[None], sh_p[None],
            k_s[None], v_s[None], s_s[None], sh_s[None])
```

```python
import functools

import jax
import jax.numpy as jnp
from jax import lax
from jax.experimental import pallas as pl
from jax.experimental.pallas import tpu as pltpu

F32 = jnp.float32
BF16 = jnp.bfloat16

D_MODEL = 1024
HEAD_DIM = 64
D_RWKV = 512
D_SB = 512
N_HEADS = 8
N_PAIRS = 4
LANES = 128
D_DECAY_LORA = 64
D_AAA_LORA = 64
D_GATE_LORA = 160
D_RWKV_IN = 3 * D_RWKV + D_DECAY_LORA + D_AAA_LORA + D_GATE_LORA
D_FF = 4 * D_MODEL
CHUNK = 64
RWKV_ROWS = 256
SB_SCALE = HEAD_DIM ** -0.5
LOG2E = 1.4426950408889634
Q_SCALE = SB_SCALE * LOG2E
SKIP_BITS = 150.0
RMS_EPS = 1e-6
GN_EPS = 64e-5
L2_EPS = 1e-24

PW_OFF = 3 * D_RWKV
PA_OFF = PW_OFF + LANES
PG_OFF = PA_OFF + LANES
D_RWKV_PAD = PG_OFF + 2 * LANES
SB_OFF = D_RWKV_PAD
GATE_OFF = SB_OFF + 3 * D_SB
D_IN_PAD = GATE_OFF + 2 * D_MODEL

VMEM_LIMIT = 56 * 1024 * 1024


def _dot(a, b):
    return jnp.dot(a, b, preferred_element_type=F32)


def _dot_nt(a, b):
    return lax.dot_general(a, b, (((1,), (1,)), ((), ())), preferred_element_type=F32)


def _split(x):
    hi = x.astype(BF16)
    lo = (x - hi.astype(F32)).astype(BF16)
    return hi, lo


def _dot_hl(x, m):
    hi, lo = _split(x)
    return _dot(hi, m) + _dot(lo, m)


def _dot_lh(m, x):
    hi, lo = _split(x)
    return _dot(m, hi) + _dot(m, lo)


def _head_sum(x, bd):
    xb = x.astype(BF16)
    return jnp.concatenate([_dot(xb[:, pr * LANES:(pr + 1) * LANES], bd) for pr in range(N_PAIRS)],
                           axis=1)


def _mm(a, b):
    return _dot(a.astype(BF16), b.astype(BF16))


def _mm_nt(a, b):
    return _dot_nt(a.astype(BF16), b.astype(BF16))


def _softplus(z):
    return jnp.maximum(z, 0.0) + jnp.log(1.0 + jnp.exp(-jnp.abs(z)))


def _softplus2(z):
    neg_abs = lax.bitcast_convert_type(lax.bitcast_convert_type(z, jnp.uint32) | jnp.uint32(1 << 31), F32)
    return jnp.maximum(z, 0.0) + jnp.log2(1.0 + jnp.exp2(neg_abs))


def _sigmoid(z):
    return 1.0 / (1.0 + jnp.exp(-z))


def _inproj_kernel(x_ref, g1_ref, w_ref, gq_ref, gk_ref, bd_ref,
                   pr_ref, q_ref, kh_ref, vh_ref, kb_ref, vb_ref, gate_ref):
    x = x_ref[0]
    ms = jnp.mean(x * x, axis=-1, keepdims=True)
    xn = (x * lax.rsqrt(ms + RMS_EPS) * g1_ref[...]).astype(BF16)
    pr_ref[0] = _dot(xn, w_ref[:, 0:D_RWKV_PAD])
    bd = bd_ref[...]

    def head_norm(u, g):
        ss = _head_sum(u * u, bd) * (1.0 / HEAD_DIM)
        return u * lax.rsqrt(ss + RMS_EPS) * g

    q = head_norm(_dot(xn, w_ref[:, SB_OFF:SB_OFF + D_SB]), gq_ref[...])
    k = head_norm(_dot(xn, w_ref[:, SB_OFF + D_SB:SB_OFF + 2 * D_SB]), gk_ref[...])
    v = _dot(xn, w_ref[:, SB_OFF + 2 * D_SB:SB_OFF + 3 * D_SB])
    q_ref[0] = (q * Q_SCALE).astype(BF16)
    kb_ref[0] = k.astype(BF16)
    vb_ref[0] = v.astype(BF16)
    for h in range(N_HEADS):
        kh_ref[0, h] = k[:, h * HEAD_DIM:(h + 1) * HEAD_DIM]
        vh_ref[0, h] = v[:, h * HEAD_DIM:(h + 1) * HEAD_DIM]
    gate_ref[0] = _sigmoid(_dot(xn, w_ref[:, GATE_OFF:GATE_OFF + 2 * D_MODEL]))


def _inproj(x, g1, w_all, gq, gk, bd, tm):
    b, t, _ = x.shape
    row = lambda w: pl.BlockSpec((1, tm, w), lambda i, j: (i, j, 0))
    heads = pl.BlockSpec((1, N_HEADS, tm, HEAD_DIM), lambda i, j: (i, 0, j, 0))
    const = lambda a: pl.BlockSpec(a.shape, lambda i, j: (0, 0))
    return pl.pallas_call(
        _inproj_kernel,
        grid=(b, t // tm),
        in_specs=[row(D_MODEL), const(g1), const(w_all), const(gq), const(gk), const(bd)],
        out_specs=[row(D_RWKV_PAD), row(D_SB), heads, heads, row(D_SB), row(D_SB),
                   row(2 * D_MODEL)],
        out_shape=[jax.ShapeDtypeStruct((b, t, D_RWKV_PAD), F32),
                   jax.ShapeDtypeStruct((b, t, D_SB), BF16),
                   jax.ShapeDtypeStruct((b, N_HEADS, t, HEAD_DIM), F32),
                   jax.ShapeDtypeStruct((b, N_HEADS, t, HEAD_DIM), F32),
                   jax.ShapeDtypeStruct((b, t, D_SB), BF16),
                   jax.ShapeDtypeStruct((b, t, D_SB), BF16),
                   jax.ShapeDtypeStruct((b, t, 2 * D_MODEL), F32)],
        compiler_params=pltpu.CompilerParams(dimension_semantics=("parallel", "parallel"),
                                             vmem_limit_bytes=VMEM_LIMIT),
        name="inproj",
    )(x, g1, w_all, gq, gk, bd)


def _rwkv_kernel(p_ref, shift_ref, s0_ref, mu_ref, w0_ref, w2_ref, a0_ref, a2_ref, g2_ref,
                 kk_ref, ka_ref, rk_ref, lw_ref, lb_ref, bd_ref, tri_ref,
                 y_ref, sout_ref, carry_sc, s_sc, *, bb, tt):
    c = pl.program_id(1)
    rows = bb * tt

    @pl.when(c == 0)
    def _():
        carry_sc[...] = shift_ref[...]
        s_sc[...] = s0_ref[...]

    p = p_ref[...].reshape(rows, D_RWKV_PAD)
    row = lax.broadcasted_iota(jnp.int32, p.shape, 0)
    prev = pltpu.roll(p, 1, 0)
    for b_ in range(bb):
        prev = jnp.where(row == b_ * tt, carry_sc[b_], prev)
        carry_sc[b_] = p[(b_ + 1) * tt - 1:(b_ + 1) * tt, :]
    pm = p + (prev - p) * mu_ref[...]
    r = pm[:, 0:D_RWKV]
    k = pm[:, D_RWKV:2 * D_RWKV]
    v = pm[:, 2 * D_RWKV:3 * D_RWKV]
    wl = pm[:, PW_OFF:PW_OFF + LANES]
    al = pm[:, PA_OFF:PA_OFF + LANES]
    gl = pm[:, PG_OFF:PG_OFF + 2 * LANES]

    bd = bd_ref[...]
    w = -_softplus(-(w0_ref[...] + _mm(jnp.tanh(wl), w2_ref[...]))) - 0.5
    ld = -jnp.exp(w)
    a = _sigmoid(a0_ref[...] + _mm(al, a2_ref[...]))
    g = _mm(_sigmoid(gl), g2_ref[...])
    kk = k * kk_ref[...]
    kk = kk * lax.rsqrt(jnp.maximum(_head_sum(kk * kk, bd), L2_EPS))
    k = k * (1.0 + (a - 1.0) * ka_ref[...])
    bonus = _head_sum(r * k * rk_ref[...], bd) * v

    n_ch = tt // CHUNK
    chunks = [(b_, ch) for b_ in range(bb) for ch in range(n_ch)]
    rows_of = lambda b_, ch: slice(b_ * tt + ch * CHUNK, b_ * tt + (ch + 1) * CHUNK)
    cl = _dot_lh(tri_ref[...], ld)
    cl_end = jnp.concatenate(
        [jnp.broadcast_to(cl[rows_of(*c).stop - 1:rows_of(*c).stop, :], (CHUNK, D_RWKV))
         for c in chunks], axis=0)
    w_inv = jnp.exp(-cl)
    w_end = jnp.exp(cl_end - cl)
    at = -kk * jnp.exp(cl - ld)
    bt = kk * a * w_inv
    kt = k * w_inv
    rt = r * jnp.exp(cl)
    be = kk * a * w_end
    ke = k * w_end
    wc = jnp.exp(cl_end)

    lane = lax.broadcasted_iota(jnp.int32, (CHUNK, LANES), 1)
    m0 = lane < HEAD_DIM
    ri = lax.broadcasted_iota(jnp.int32, (LANES, LANES), 0)
    ci = lax.broadcasted_iota(jnp.int32, (LANES, LANES), 1)
    same = (ri // CHUNK) == (ci // CHUNK)
    strict = same & ((ri % CHUNK) > (ci % CHUNK))
    incl = same & ((ri % CHUNK) >= (ci % CHUNK))
    eye = (ri == ci).astype(F32)

    def stack(u):
        return jnp.concatenate([jnp.where(m0, u, 0.0), jnp.where(m0, 0.0, u)], axis=0)

    keys = [(b_, ch, pr) for (b_, ch) in chunks for pr in range(N_PAIRS)]
    blk = lambda x, key: x[rows_of(key[0], key[1]), key[2] * LANES:(key[2] + 1) * LANES]
    la = {q: stack(blk(at, q)) for q in keys}
    lr = {q: stack(blk(rt, q)) for q in keys}
    vst = {q: stack(blk(v, q)) for q in keys}
    g4 = {q: _mm_nt(jnp.concatenate([la[q], lr[q]], axis=0),
                    jnp.concatenate([stack(blk(bt, q)), stack(blk(kt, q))], axis=0)) for q in keys}
    a_ab = {q: jnp.where(strict, g4[q][:LANES, :LANES], 0.0) for q in keys}
    a_ak = {q: jnp.where(strict, g4[q][:LANES, LANES:], 0.0) for q in keys}
    a_rb = {q: jnp.where(incl, g4[q][LANES:, :LANES], 0.0) for q in keys}
    a_rk = {q: jnp.where(incl, g4[q][LANES:, LANES:], 0.0) for q in keys}
    tinv = {q: eye + a_ab[q] for q in keys}
    apow = {q: _mm(a_ab[q], a_ab[q]) for q in keys}
    for _ in range(4):
        nxt = {q: _mm(apow[q], jnp.concatenate([apow[q], tinv[q]], axis=1)) for q in keys}
        apow = {q: nxt[q][:, :LANES] for q in keys}
        tinv = {q: tinv[q] + nxt[q][:, LANES:] for q in keys}
    tinv = {q: tinv[q] + _mm(apow[q], tinv[q]) for q in keys}
    akv = {q: _mm(jnp.concatenate([a_ak[q], a_rk[q]], axis=0), vst[q]) for q in keys}
    tlav = {q: _mm(tinv[q], jnp.concatenate([la[q], akv[q][:LANES]], axis=1)) for q in keys}
    rlon = {q: _mm(a_rb[q], tlav[q]) for q in keys}
    rl = {q: lr[q] + rlon[q][:, :LANES] for q in keys}
    on = {q: rlon[q][:, LANES:] + akv[q][LANES:] for q in keys}
    mn = {q: _mm(tlav[q].T, stack(blk(be, q))) for q in keys}
    nc = {q: mn[q][LANES:] + _mm(vst[q].T, stack(blk(ke, q))) for q in keys}

    out_rows = []
    for b_ in range(bb):
        state = [s_sc[b_, pr] for pr in range(N_PAIRS)]
        for ch in range(n_ch):
            outs = []
            for pr in range(N_PAIRS):
                q = (b_, ch, pr)
                s_b = state[pr].astype(BF16)
                o_st = _dot_nt(rl[q].astype(BF16), s_b) + on[q]
                outs.append(o_st[0:CHUNK] + o_st[CHUNK:2 * CHUNK])
                state[pr] = (state[pr] * blk(wc, q)[0:1, :] + _dot(s_b, mn[q][:LANES].astype(BF16))
                             + nc[q])
            out_rows.append(jnp.concatenate(outs, axis=1))
        for pr in range(N_PAIRS):
            s_sc[b_, pr] = state[pr]
            sout_ref[b_, pr] = state[pr]
    o = jnp.concatenate(out_rows, axis=0)

    mean = _head_sum(o, bd) * (1.0 / HEAD_DIM)
    oc = o - mean
    var = _head_sum(oc * oc, bd) * (1.0 / HEAD_DIM)
    o = oc * lax.rsqrt(var + GN_EPS)
    y_ref[...] = ((o * lw_ref[...] + lb_ref[...] + bonus) * g).reshape(bb, tt, D_RWKV)


def _rwkv(p3d, shift_pad, s0_bd, params, bd, tri, bb, tt):
    b, t, _ = p3d.shape
    const = lambda a: pl.BlockSpec(a.shape, lambda i, j: (0,) * a.ndim)
    state = pl.BlockSpec((bb, N_PAIRS, LANES, LANES), lambda i, j: (i, 0, 0, 0))
    return pl.pallas_call(
        functools.partial(_rwkv_kernel, bb=bb, tt=tt),
        grid=(b // bb, t // tt),
        in_specs=[pl.BlockSpec((bb, tt, D_RWKV_PAD), lambda i, j: (i, j, 0)),
                  pl.BlockSpec((bb, 1, D_RWKV_PAD), lambda i, j: (i, 0, 0)), state]
                 + [const(a) for a in params] + [const(bd), const(tri)],
        out_specs=[pl.BlockSpec((bb, tt, D_RWKV), lambda i, j: (i, j, 0)), state],
        out_shape=[jax.ShapeDtypeStruct((b, t, D_RWKV), F32),
                   jax.ShapeDtypeStruct((b, N_PAIRS, LANES, LANES), F32)],
        scratch_shapes=[pltpu.VMEM((bb, 1, D_RWKV_PAD), F32),
                        pltpu.VMEM((bb, N_PAIRS, LANES, LANES), F32)],
        compiler_params=pltpu.CompilerParams(dimension_semantics=("parallel", "arbitrary"),
                                             vmem_limit_bytes=VMEM_LIMIT),
        name="rwkv",
    )(p3d, shift_pad, s0_bd, *params, bd, tri)


def _sb_prompt_kernel(q_ref, k_ref, v_ref, m2_ref, y_ref, qm_sc, acc_sc, carry_sc, *, tq, tk):
    i = pl.program_id(1)
    nd = tq // tk
    lane_q = lax.broadcasted_iota(jnp.int32, (tq, LANES), 1) < HEAD_DIM
    lane_k = lax.broadcasted_iota(jnp.int32, (tk, LANES), 1) < HEAD_DIM
    qpos = i * tq + lax.broadcasted_iota(jnp.int32, (tq, tk), 0)
    kcol = lax.broadcasted_iota(jnp.int32, (tq, tk), 1)
    for pr in range(N_PAIRS):
        qp = q_ref[0, :, pr * LANES:(pr + 1) * LANES]
        qm_sc[2 * pr] = jnp.where(lane_q, qp, jnp.zeros_like(qp))
        qm_sc[2 * pr + 1] = jnp.where(lane_q, jnp.zeros_like(qp), qp)
    acc_sc[...] = jnp.zeros_like(acc_sc)
    carry_sc[...] = jnp.zeros_like(carry_sc)

    def block(j, masked):
        start = pl.multiple_of(j * tk, tk)
        mask = (j * tk + kcol < qpos) if masked else None
        m2 = m2_ref[...]
        heads = range(N_HEADS)
        z = [_dot_nt(qm_sc[h], k_ref[0, pl.ds(start, tk), (h // 2) * LANES:(h // 2 + 1) * LANES])
             for h in heads]
        sp = [_softplus2(z[h]) for h in heads]
        if masked:
            sp = [jnp.where(mask, s, 0.0) for s in sp]
        hl = [jnp.concatenate(_split(s), axis=1) for s in sp]
        tail = [_dot(hl[h], m2) + carry_sc[h] for h in heads]
        att = [jnp.exp2(z[h] - tail[h]) for h in heads]
        if masked:
            att = [jnp.where(mask, a, 0.0) for a in att]
        for h in heads:
            carry_sc[h] = jnp.broadcast_to(tail[h][:, 0:1], (tq, tk))
        for pr in range(N_PAIRS):
            vblk = v_ref[0, pl.ds(start, tk), pr * LANES:(pr + 1) * LANES]
            v2 = jnp.concatenate([jnp.where(lane_k, vblk, jnp.zeros_like(vblk)),
                                  jnp.where(lane_k, jnp.zeros_like(vblk), vblk)], axis=0)
            att2 = jnp.concatenate([att[2 * pr].astype(BF16), att[2 * pr + 1].astype(BF16)], axis=1)
            acc_sc[pr] += _dot(att2, v2)

    def min_mass():
        m = carry_sc[0]
        for h in range(1, N_HEADS):
            m = jnp.minimum(m, carry_sc[h])
        return jnp.min(m)

    for d in reversed(range(nd)):
        block(i * nd + d, True)

    def body(state):
        jj, _ = state
        block(i * nd - 1 - jj, False)
        return jj + 1, min_mass()

    lax.while_loop(lambda s: jnp.logical_and(s[0] < i * nd, s[1] < SKIP_BITS), body,
                   (jnp.int32(0), min_mass()))
    for pr in range(N_PAIRS):
        y_ref[0, :, pr * LANES:(pr + 1) * LANES] = acc_sc[pr]


def _sb_prompt(qb, kb, vb, m2, tq, tk):
    b, t, _ = qb.shape
    full = pl.BlockSpec((1, t, D_SB), lambda i, j: (i, 0, 0))
    tile = pl.BlockSpec((1, tq, D_SB), lambda i, j: (i, j, 0))
    return pl.pallas_call(
        functools.partial(_sb_prompt_kernel, tq=tq, tk=tk),
        grid=(b, t // tq),
        in_specs=[tile, full, full, pl.BlockSpec(m2.shape, lambda i, j: (0, 0))],
        out_specs=tile,
        out_shape=jax.ShapeDtypeStruct((b, t, D_SB), F32),
        scratch_shapes=[pltpu.VMEM((N_HEADS, tq, LANES), BF16),
                        pltpu.VMEM((N_PAIRS, tq, LANES), F32),
                        pltpu.VMEM((N_HEADS, tq, tk), F32)],
        compiler_params=pltpu.CompilerParams(dimension_semantics=("parallel", "arbitrary"),
                                             vmem_limit_bytes=VMEM_LIMIT),
        name="sb_prompt",
    )(qb, kb, vb, m2)


def _sb_sample_kernel(q_ref, kn_ref, vn_ref, kc_ref, vc_ref, m2_ref, m2n_ref, y_ref, *, tk):
    t = q_ref.shape[2]
    past = kc_ref.shape[2]
    m2, m2n = m2_ref[...], m2n_ref[...]
    causal = (lax.broadcasted_iota(jnp.int32, (t, t), 1) < lax.broadcasted_iota(jnp.int32, (t, t), 0))
    heads = range(N_HEADS)
    nb = past // tk
    zn = [_dot_nt(q_ref[0, h], kn_ref[0, h].astype(BF16)) for h in heads]
    zp = [_dot_nt(q_ref[0, h], kc_ref[0, h].astype(BF16)) for h in heads]
    spn = [jnp.where(causal, _softplus2(z), 0.0) for z in zn]
    spp = [_softplus2(z) for z in zp]
    tailn = [_dot_hl(s, m2n) for s in spn]
    tailp = [[_dot(jnp.concatenate(_split(s[:, c * tk:(c + 1) * tk]), axis=1), m2)
              for c in range(nb)] for s in spp]
    for h in heads:
        off = jnp.broadcast_to(tailn[h][:, 0:1], (t, tk))
        for c in reversed(range(nb)):
            tailp[h][c] = tailp[h][c] + off
            off = jnp.broadcast_to(tailp[h][c][:, 0:1], (t, tk))
    for h in heads:
        attn = jnp.where(causal, jnp.exp2(zn[h] - tailn[h]), 0.0).astype(BF16)
        attp = jnp.exp2(zp[h] - jnp.concatenate(tailp[h], axis=1)).astype(BF16)
        y_ref[0, h] = _dot(attn, vn_ref[0, h].astype(BF16)) + _dot(attp, vc_ref[0, h].astype(BF16))


def _sb_sample(qh, knh, vnh, kc, vc, m2, m2n, tk):
    b, h, t, d = qh.shape
    past = kc.shape[2]
    new = pl.BlockSpec((1, h, t, d), lambda i: (i, 0, 0, 0))
    old = pl.BlockSpec((1, h, past, d), lambda i: (i, 0, 0, 0))
    const = lambda a: pl.BlockSpec(a.shape, lambda i: (0, 0))
    return pl.pallas_call(
        functools.partial(_sb_sample_kernel, tk=tk),
        grid=(b,),
        in_specs=[new, new, new, old, old, const(m2), const(m2n)],
        out_specs=new,
        out_shape=jax.ShapeDtypeStruct((b, h, t, d), F32),
        compiler_params=pltpu.CompilerParams(dimension_semantics=("parallel",),
                                             vmem_limit_bytes=VMEM_LIMIT),
        name="sb_sample",
    )(qh, knh, vnh, kc, vc, m2, m2n)


def _merge_ffn_kernel(x_ref, ya_ref, yb_ref, gate_ref, wua_ref, wub_ref, wo_ref, g2_ref,
                      w1_ref, w2_ref, y_ref, *, ff_chunk):
    gate = gate_ref[...]
    merged = (gate[:, 0:D_MODEL] * _dot(ya_ref[...].astype(BF16), wua_ref[...])
              + gate[:, D_MODEL:2 * D_MODEL] * _dot(yb_ref[...].astype(BF16), wub_ref[...]))
    x = x_ref[...] + _dot(merged.astype(BF16), wo_ref[...])
    ms = jnp.mean(x * x, axis=-1, keepdims=True)
    xn = (x * lax.rsqrt(ms + RMS_EPS) * g2_ref[...]).astype(BF16)
    for c in range(D_FF // ff_chunk):
        sl = slice(c * ff_chunk, (c + 1) * ff_chunk)
        h = jnp.maximum(_dot(xn, w1_ref[:, sl]), 0.0)
        x = x + _dot((h * h).astype(BF16), w2_ref[sl, :])
    y_ref[...] = x


def _merge_ffn(x2d, ya, yb, gate, wua, wub, wo, g2, w1, w2, tm, ff_chunk):
    n = x2d.shape[0]
    row = lambda w: pl.BlockSpec((tm, w), lambda i: (i, 0))
    const = lambda a: pl.BlockSpec(a.shape, lambda i: (0, 0), pipeline_mode=pl.Buffered(1))
    return pl.pallas_call(
        functools.partial(_merge_ffn_kernel, ff_chunk=ff_chunk),
        grid=(n // tm,),
        in_specs=[row(D_MODEL), row(D_RWKV), row(D_SB), row(2 * D_MODEL),
                  const(wua), const(wub), const(wo), const(g2), const(w1), const(w2)],
        out_specs=row(D_MODEL),
        out_shape=jax.ShapeDtypeStruct((n, D_MODEL), F32),
        compiler_params=pltpu.CompilerParams(dimension_semantics=("parallel",),
                                             vmem_limit_bytes=VMEM_LIMIT),
        name="merge_ffn",
    )(x2d, ya, yb, gate, wua, wub, wo, g2, w1, w2)


def _pad_cols(a, width):
    return jnp.pad(a, [(0, 0)] * (a.ndim - 1) + [(0, width - a.shape[-1])])


def _pad_rwkv_cols(a):
    o1 = 3 * D_RWKV + D_DECAY_LORA
    o2 = o1 + D_AAA_LORA
    return jnp.concatenate([a[..., :3 * D_RWKV],
                            _pad_cols(a[..., 3 * D_RWKV:o1], LANES),
                            _pad_cols(a[..., o1:o2], LANES),
                            _pad_cols(a[..., o2:], 2 * LANES)], axis=-1)


def _unpad_rwkv_cols(a):
    return jnp.concatenate([a[..., :PW_OFF + D_DECAY_LORA],
                            a[..., PA_OFF:PA_OFF + D_AAA_LORA],
                            a[..., PG_OFF:PG_OFF + D_GATE_LORA]], axis=-1)


def _state_to_pairs(s):
    b = s.shape[0]
    s = s.reshape(b, N_PAIRS, 2, HEAD_DIM, HEAD_DIM)
    z = jnp.zeros_like(s[:, :, 0])
    top = jnp.concatenate([s[:, :, 0], z], axis=-1)
    bot = jnp.concatenate([z, s[:, :, 1]], axis=-1)
    return jnp.concatenate([top, bot], axis=-2)


def _pairs_to_state(sp):
    b = sp.shape[0]
    h0 = sp[:, :, :HEAD_DIM, :HEAD_DIM]
    h1 = sp[:, :, HEAD_DIM:, HEAD_DIM:]
    return jnp.stack([h0, h1], axis=2).reshape(b, N_HEADS, HEAD_DIM, HEAD_DIM)


def _to_heads(a):
    b, t, _ = a.shape
    return a.reshape(b, t, N_HEADS, HEAD_DIM).transpose(0, 2, 1, 3)


def _tri_ge(n):
    i = jnp.arange(n)
    return (i[:, None] >= i[None, :]).astype(BF16)


def _layer(x, shift_prev, s0, k_past, v_past, wts, consts, tm, tq, tk):
    bsz, t, _ = x.shape
    tt = min(t, RWKV_ROWS)
    bb = RWKV_ROWS // tt
    n = bsz * t
    x2d = x.reshape(n, D_MODEL)
    if t % tm == 0:
        pr3, qb, k_new, v_new, kb, vb, gate = _inproj(x, wts["g1"], wts["w_all"], wts["gq"],
                                                      wts["gk"], consts["bd"], tm)
    else:
        pr3, qb, k_new, v_new, kb, vb, gate = _inproj(x2d[None], wts["g1"], wts["w_all"], wts["gq"],
                                                      wts["gk"], consts["bd"], tm)
        pr3, qb = pr3.reshape(bsz, t, D_RWKV_PAD), qb.reshape(bsz, t, D_SB)
        unflat = lambda a: a.reshape(N_HEADS, bsz, t, HEAD_DIM).transpose(1, 0, 2, 3)
        k_new, v_new = unflat(k_new), unflat(v_new)
    gate = gate.reshape(n, 2 * D_MODEL)
    shift_new = _unpad_rwkv_cols(pr3[:, t - 1:t, :])
    ya, s_pairs = _rwkv(pr3, _pad_rwkv_cols(shift_prev), _state_to_pairs(s0.astype(F32)),
                        wts["rwkv"], consts["bd"], consts["tri"], bb, tt)
    if k_past is None:
        yb = _sb_prompt(qb, kb, vb, consts["m2"], tq, tk)
    else:
        ybh = _sb_sample(_to_heads(qb), k_new, v_new, k_past, v_past,
                         consts["m2"], consts["m2n"], tk)
        yb = ybh.transpose(0, 2, 1, 3).reshape(bsz, t, D_SB)
    y = _merge_ffn(x2d, ya.reshape(n, D_RWKV), yb.reshape(n, D_SB), gate, wts["wua"], wts["wub"],
                   wts["wo"], wts["g2"], wts["w1"], wts["w2"], tm, 1024)
    return y.reshape(bsz, t, D_MODEL), shift_new, _pairs_to_state(s_pairs), k_new, v_new


def _prep_weights(g_norm1, w_in, rwkv_mu, rwkv_w0, rwkv_w2, rwkv_a0, rwkv_a2, rwkv_g2, rwkv_k_k,
                  rwkv_k_a, rwkv_r_k, rwkv_lnx_w, rwkv_lnx_b, sb_q_norm_g, sb_k_norm_g, w_up_a,
                  w_up_b, w_out, g_norm2, w_ff1, w_ff2):
    row = lambda a: a.reshape(1, -1).astype(F32)
    w_all = jnp.concatenate([_pad_rwkv_cols(w_in[:, :D_RWKV_IN]), w_in[:, D_RWKV_IN:]], axis=1)
    pad_rows = lambda a, rows: jnp.pad(a, ((0, rows - a.shape[0]), (0, 0))).astype(BF16)
    rwkv = (row(_pad_rwkv_cols(rwkv_mu)), row(rwkv_w0), pad_rows(rwkv_w2, LANES), row(rwkv_a0),
            pad_rows(rwkv_a2, LANES), pad_rows(rwkv_g2, 2 * LANES), row(rwkv_k_k), row(rwkv_k_a),
            row(rwkv_r_k), row(rwkv_lnx_w), row(rwkv_lnx_b))
    return {
        "g1": row(g_norm1), "w_all": w_all.astype(BF16),
        "gq": row(jnp.tile(sb_q_norm_g, N_HEADS)), "gk": row(jnp.tile(sb_k_norm_g, N_HEADS)),
        "rwkv": rwkv,
        "wua": w_up_a.astype(BF16), "wub": w_up_b.astype(BF16), "wo": w_out.astype(BF16),
        "g2": row(g_norm2), "w1": w_ff1.astype(BF16), "w2": w_ff2.astype(BF16),
    }


def _constants(tk, t_new, rwkv_rows):
    hd = jnp.arange(LANES) // HEAD_DIM
    tri = _tri_ge(tk)
    trin = _tri_ge(t_new)
    ch = jnp.arange(rwkv_rows) // CHUNK
    return {
        "bd": (hd[:, None] == hd[None, :]).astype(BF16),
        "tri": _tri_ge(rwkv_rows) * (ch[:, None] == ch[None, :]).astype(BF16),
        "m2": jnp.concatenate([tri, tri], axis=0),
        "m2n": trin,
    }


def kernel(x_prompt, x_sample, cache_sb_k, cache_sb_v, state_rwkv_wkv, state_rwkv_shift, g_norm1, w_in, rwkv_mu, rwkv_w0, rwkv_w2, rwkv_a0, rwkv_a2, rwkv_g2, rwkv_k_k, rwkv_k_a, rwkv_r_k, rwkv_lnx_w, rwkv_lnx_b, sb_q_norm_g, sb_k_norm_g, w_up_a, w_up_b, w_out, g_norm2, w_ff1, w_ff2):
    depth = w_in.shape[0]
    assert depth == 1
    layer_w = (g_norm1, w_in, rwkv_mu, rwkv_w0, rwkv_w2, rwkv_a0, rwkv_a2, rwkv_g2, rwkv_k_k,
               rwkv_k_a, rwkv_r_k, rwkv_lnx_w, rwkv_lnx_b, sb_q_norm_g, sb_k_norm_g, w_up_a,
               w_up_b, w_out, g_norm2, w_ff1, w_ff2)
    wts = _prep_weights(*(a[0] for a in layer_w))
    tk = 128
    consts = _constants(tk, x_sample.shape[1], RWKV_ROWS)
    bp = x_prompt.shape[0]
    s0_p = jnp.zeros((bp, N_HEADS, HEAD_DIM, HEAD_DIM), F32)
    shift0_p = jnp.zeros((bp, 1, D_RWKV_IN), x_prompt.dtype)
    y_p, sh_p, s_p, k_p, v_p = _layer(x_prompt, shift0_p, s0_p, None, None, wts, consts,
                                      256, 256, tk)
    y_s, sh_s, s_s, k_s, v_s = _layer(x_sample, state_rwkv_shift[0], state_rwkv_wkv[0],
                                      cache_sb_k[0], cache_sb_v[0], wts, consts, 256, 256, tk)
    return (y_p, y_s, k_p[None], v_p[None], s_p[None], sh_p[None],
            k_s[None], v_s[None], s_s[None], sh_s[None])
```

```python
import functools

import jax
import jax.numpy as jnp
from jax import lax
from jax.experimental import pallas as pl
from jax.experimental.pallas import tpu as pltpu

F32 = jnp.float32
BF16 = jnp.bfloat16

D_MODEL = 1024
HEAD_DIM = 64
D_RWKV = 512
D_SB = 512
N_HEADS = 8
N_PAIRS = 4
LANES = 128
D_DECAY_LORA = 64
D_AAA_LORA = 64
D_GATE_LORA = 160
D_RWKV_IN = 3 * D_RWKV + D_DECAY_LORA + D_AAA_LORA + D_GATE_LORA
D_FF = 4 * D_MODEL
CHUNK = 64
RWKV_ROWS = 256
TILE = 256
FF_CHUNK = 1024
SB_SCALE = HEAD_DIM ** -0.5
LOG2E = 1.4426950408889634
Q_SCALE = SB_SCALE * LOG2E
SKIP_BITS = 150.0
RMS_EPS = 1e-6
GN_EPS = 64e-5
L2_EPS = 1e-24

PW_OFF = 3 * D_RWKV
PA_OFF = PW_OFF + LANES
PG_OFF = PA_OFF + LANES
D_RWKV_PAD = PG_OFF + 2 * LANES
SB_OFF = D_RWKV_PAD
GATE_OFF = SB_OFF + 3 * D_SB
D_IN_PAD = GATE_OFF + 2 * D_MODEL

VMEM_LIMIT = 56 * 1024 * 1024


def _dot(a, b):
    return jnp.dot(a, b, preferred_element_type=F32)


def _dot_nt(a, b):
    return lax.dot_general(a, b, (((1,), (1,)), ((), ())), preferred_element_type=F32)


def _split(x):
    hi = x.astype(BF16)
    lo = (x - hi.astype(F32)).astype(BF16)
    return hi, lo


def _dot_hl(x, m):
    hi, lo = _split(x)
    return _dot(hi, m) + _dot(lo, m)


def _dot_lh(m, x):
    hi, lo = _split(x)
    return _dot(m, hi) + _dot(m, lo)


def _head_sum(x, bd):
    xb = x.astype(BF16)
    w = bd.shape[0]
    return jnp.concatenate([_dot(xb[:, c * w:(c + 1) * w], bd) for c in range(x.shape[1] // w)],
                           axis=1)


def _mm(a, b):
    return _dot(a.astype(BF16), b.astype(BF16))


def _mm_nt(a, b):
    return _dot_nt(a.astype(BF16), b.astype(BF16))


def _softplus(z):
    return jnp.maximum(z, 0.0) + jnp.log(1.0 + jnp.exp(-jnp.abs(z)))


def _softplus2(z):
    neg_abs = lax.bitcast_convert_type(lax.bitcast_convert_type(z, jnp.uint32) | jnp.uint32(1 << 31), F32)
    return jnp.maximum(z, 0.0) + jnp.log2(1.0 + jnp.exp2(neg_abs))


def _sigmoid(z):
    return 1.0 / (1.0 + jnp.exp(-z))


def _inproj_kernel(x_ref, g1_ref, w_ref, gq_ref, gk_ref, bd_ref,
                   pr_ref, q_ref, kh_ref, vh_ref, kb_ref, vb_ref, gate_ref):
    x = x_ref[0]
    ms = jnp.mean(x * x, axis=-1, keepdims=True)
    xn = (x * lax.rsqrt(ms + RMS_EPS) * g1_ref[...]).astype(BF16)
    pr_ref[0] = _dot(xn, w_ref[:, 0:D_RWKV_PAD])
    bd = bd_ref[...]

    def head_norm(u, g):
        ss = _head_sum(u * u, bd) * (1.0 / HEAD_DIM)
        return u * lax.rsqrt(ss + RMS_EPS) * g

    q = head_norm(_dot(xn, w_ref[:, SB_OFF:SB_OFF + D_SB]), gq_ref[...])
    k = head_norm(_dot(xn, w_ref[:, SB_OFF + D_SB:SB_OFF + 2 * D_SB]), gk_ref[...])
    v = _dot(xn, w_ref[:, SB_OFF + 2 * D_SB:SB_OFF + 3 * D_SB])
    q_ref[0] = (q * Q_SCALE).astype(BF16)
    kb_ref[0] = k.astype(BF16)
    vb_ref[0] = v.astype(BF16)
    kh_ref[0] = k.T.reshape(N_HEADS, HEAD_DIM, k.shape[0])
    vh_ref[0] = v.T.reshape(N_HEADS, HEAD_DIM, v.shape[0])
    gate_ref[0] = _sigmoid(_dot(xn, w_ref[:, GATE_OFF:GATE_OFF + 2 * D_MODEL]))


def _inproj(x, g1, w_all, gq, gk, bd, tm):
    b, t, _ = x.shape
    row = lambda w: pl.BlockSpec((1, tm, w), lambda i, j: (i, j, 0))
    heads = pl.BlockSpec((1, N_HEADS, HEAD_DIM, tm), lambda i, j: (i, 0, 0, j))
    const = lambda a: pl.BlockSpec(a.shape, lambda i, j: (0, 0))
    return pl.pallas_call(
        _inproj_kernel,
        grid=(b, t // tm),
        in_specs=[row(D_MODEL), const(g1), const(w_all), const(gq), const(gk), const(bd)],
        out_specs=[row(D_RWKV_PAD), row(D_SB), heads, heads, row(D_SB), row(D_SB),
                   row(2 * D_MODEL)],
        out_shape=[jax.ShapeDtypeStruct((b, t, D_RWKV_PAD), F32),
                   jax.ShapeDtypeStruct((b, t, D_SB), BF16),
                   jax.ShapeDtypeStruct((b, N_HEADS, HEAD_DIM, t), F32),
                   jax.ShapeDtypeStruct((b, N_HEADS, HEAD_DIM, t), F32),
                   jax.ShapeDtypeStruct((b, t, D_SB), BF16),
                   jax.ShapeDtypeStruct((b, t, D_SB), BF16),
                   jax.ShapeDtypeStruct((b, t, 2 * D_MODEL), F32)],
        compiler_params=pltpu.CompilerParams(dimension_semantics=("parallel", "parallel"),
                                             vmem_limit_bytes=VMEM_LIMIT),
        name="inproj",
    )(x, g1, w_all, gq, gk, bd)


def _rwkv_kernel(p_ref, shift_ref, s0_ref, mu_ref, w0_ref, w2_ref, a0_ref, a2_ref, g2_ref,
                 kk_ref, ka_ref, rk_ref, lw_ref, lb_ref, bd_ref, tri_ref,
                 y_ref, sout_ref, carry_sc, s_sc, *, bb, tt):
    c = pl.program_id(1)
    rows = bb * tt

    @pl.when(c == 0)
    def _():
        carry_sc[...] = shift_ref[...]
        s_sc[...] = s0_ref[...]

    p = p_ref[...].reshape(rows, D_RWKV_PAD)
    row = lax.broadcasted_iota(jnp.int32, p.shape, 0)
    prev = pltpu.roll(p, 1, 0)
    for b_ in range(bb):
        prev = jnp.where(row == b_ * tt, carry_sc[b_], prev)
        carry_sc[b_] = p[(b_ + 1) * tt - 1:(b_ + 1) * tt, :]
    pm = p + (prev - p) * mu_ref[...]
    r = pm[:, 0:D_RWKV]
    k = pm[:, D_RWKV:2 * D_RWKV]
    v = pm[:, 2 * D_RWKV:3 * D_RWKV]
    wl = pm[:, PW_OFF:PW_OFF + LANES]
    al = pm[:, PA_OFF:PA_OFF + LANES]
    gl = pm[:, PG_OFF:PG_OFF + 2 * LANES]

    bd = bd_ref[...]
    w = -_softplus(-(w0_ref[...] + _mm(jnp.tanh(wl), w2_ref[...]))) - 0.5
    ld = -jnp.exp(w)
    a = _sigmoid(a0_ref[...] + _mm(al, a2_ref[...]))
    g = _mm(_sigmoid(gl), g2_ref[...])
    kk = k * kk_ref[...]
    kk = kk * lax.rsqrt(jnp.maximum(_head_sum(kk * kk, bd), L2_EPS))
    k = k * (1.0 + (a - 1.0) * ka_ref[...])
    bonus = _head_sum(r * k * rk_ref[...], bd) * v

    n_ch = tt // CHUNK
    chunks = [(b_, ch) for b_ in range(bb) for ch in range(n_ch)]
    rows_of = lambda b_, ch: slice(b_ * tt + ch * CHUNK, b_ * tt + (ch + 1) * CHUNK)
    cl = _dot_lh(tri_ref[...], ld)
    cl_end = jnp.concatenate(
        [jnp.broadcast_to(cl[rows_of(*c).stop - 1:rows_of(*c).stop, :], (CHUNK, D_RWKV))
         for c in chunks], axis=0)
    w_inv = jnp.exp(-cl)
    w_end = jnp.exp(cl_end - cl)
    at = -kk * jnp.exp(cl - ld)
    bt = kk * a * w_inv
    kt = k * w_inv
    rt = r * jnp.exp(cl)
    be = kk * a * w_end
    ke = k * w_end
    wc = jnp.exp(cl_end)

    lane = lax.broadcasted_iota(jnp.int32, (CHUNK, LANES), 1)
    m0 = lane < HEAD_DIM
    ri = lax.broadcasted_iota(jnp.int32, (LANES, LANES), 0)
    ci = lax.broadcasted_iota(jnp.int32, (LANES, LANES), 1)
    same = (ri // CHUNK) == (ci // CHUNK)
    strict = same & ((ri % CHUNK) > (ci % CHUNK))
    incl = same & ((ri % CHUNK) >= (ci % CHUNK))
    eye = (ri == ci).astype(F32)

    def stack(u):
        return jnp.concatenate([jnp.where(m0, u, 0.0), jnp.where(m0, 0.0, u)], axis=0)

    keys = [(b_, ch, pr) for (b_, ch) in chunks for pr in range(N_PAIRS)]
    blk = lambda x, key: x[rows_of(key[0], key[1]), key[2] * LANES:(key[2] + 1) * LANES]
    la = {q: stack(blk(at, q)) for q in keys}
    lr = {q: stack(blk(rt, q)) for q in keys}
    vst = {q: stack(blk(v, q)) for q in keys}
    g4 = {q: _mm_nt(jnp.concatenate([la[q], lr[q]], axis=0),
                    jnp.concatenate([stack(blk(bt, q)), stack(blk(kt, q))], axis=0)) for q in keys}
    a_ab = {q: jnp.where(strict, g4[q][:LANES, :LANES], 0.0) for q in keys}
    a_ak = {q: jnp.where(strict, g4[q][:LANES, LANES:], 0.0) for q in keys}
    a_rb = {q: jnp.where(incl, g4[q][LANES:, :LANES], 0.0) for q in keys}
    a_rk = {q: jnp.where(incl, g4[q][LANES:, LANES:], 0.0) for q in keys}
    tinv = {q: eye + a_ab[q] for q in keys}
    apow = {q: _mm(a_ab[q], a_ab[q]) for q in keys}
    for _ in range(4):
        nxt = {q: _mm(apow[q], jnp.concatenate([apow[q], tinv[q]], axis=1)) for q in keys}
        apow = {q: nxt[q][:, :LANES] for q in keys}
        tinv = {q: tinv[q] + nxt[q][:, LANES:] for q in keys}
    tinv = {q: tinv[q] + _mm(apow[q], tinv[q]) for q in keys}
    akv = {q: _mm(jnp.concatenate([a_ak[q], a_rk[q]], axis=0), vst[q]) for q in keys}
    tlav = {q: _mm(tinv[q], jnp.concatenate([la[q], akv[q][:LANES]], axis=1)) for q in keys}
    rlon = {q: _mm(a_rb[q], tlav[q]) for q in keys}
    rl = {q: lr[q] + rlon[q][:, :LANES] for q in keys}
    on = {q: rlon[q][:, LANES:] + akv[q][LANES:] for q in keys}
    mn = {q: _mm(tlav[q].T, stack(blk(be, q))) for q in keys}
    nc = {q: mn[q][LANES:] + _mm(vst[q].T, stack(blk(ke, q))) for q in keys}

    out_rows = []
    for b_ in range(bb):
        state = [s_sc[b_, pr] for pr in range(N_PAIRS)]
        for ch in range(n_ch):
            outs = []
            for pr in range(N_PAIRS):
                q = (b_, ch, pr)
                s_b = state[pr].astype(BF16)
                o_st = _dot_nt(rl[q].astype(BF16), s_b) + on[q]
                outs.append(o_st[0:CHUNK] + o_st[CHUNK:2 * CHUNK])
                state[pr] = (state[pr] * blk(wc, q)[0:1, :] + _dot(s_b, mn[q][:LANES].astype(BF16))
                             + nc[q])
            out_rows.append(jnp.concatenate(outs, axis=1))
        for pr in range(N_PAIRS):
            s_sc[b_, pr] = state[pr]
            sout_ref[b_, pr] = state[pr]
    o = jnp.concatenate(out_rows, axis=0)

    mean = _head_sum(o, bd) * (1.0 / HEAD_DIM)
    oc = o - mean
    var = _head_sum(oc * oc, bd) * (1.0 / HEAD_DIM)
    o = oc * lax.rsqrt(var + GN_EPS)
    y_ref[...] = ((o * lw_ref[...] + lb_ref[...] + bonus) * g).reshape(bb, tt, D_RWKV)


def _rwkv(p3d, shift_pad, s0_bd, params, bd, tri, bb, tt):
    b, t, _ = p3d.shape
    const = lambda a: pl.BlockSpec(a.shape, lambda i, j: (0,) * a.ndim)
    state = pl.BlockSpec((bb, N_PAIRS, LANES, LANES), lambda i, j: (i, 0, 0, 0))
    return pl.pallas_call(
        functools.partial(_rwkv_kernel, bb=bb, tt=tt),
        grid=(b // bb, t // tt),
        in_specs=[pl.BlockSpec((bb, tt, D_RWKV_PAD), lambda i, j: (i, j, 0)),
                  pl.BlockSpec((bb, 1, D_RWKV_PAD), lambda i, j: (i, 0, 0)), state]
                 + [const(a) for a in params] + [const(bd), const(tri)],
        out_specs=[pl.BlockSpec((bb, tt, D_RWKV), lambda i, j: (i, j, 0)), state],
        out_shape=[jax.ShapeDtypeStruct((b, t, D_RWKV), F32),
                   jax.ShapeDtypeStruct((b, N_PAIRS, LANES, LANES), F32)],
        scratch_shapes=[pltpu.VMEM((bb, 1, D_RWKV_PAD), F32),
                        pltpu.VMEM((bb, N_PAIRS, LANES, LANES), F32)],
        compiler_params=pltpu.CompilerParams(dimension_semantics=("parallel", "arbitrary"),
                                             vmem_limit_bytes=VMEM_LIMIT),
        name="rwkv",
    )(p3d, shift_pad, s0_bd, *params, bd, tri)


def _sb_prompt_kernel(q_ref, k_ref, v_ref, tri_ref, y_ref, qm_sc, acc_sc, carry_sc, *, tq):
    i = pl.program_id(1)
    lane_q = lax.broadcasted_iota(jnp.int32, (tq, LANES), 1) < HEAD_DIM
    causal = (lax.broadcasted_iota(jnp.int32, (tq, tq), 1)
              < lax.broadcasted_iota(jnp.int32, (tq, tq), 0))
    for pr in range(N_PAIRS):
        qp = q_ref[0, :, pr * LANES:(pr + 1) * LANES]
        qm_sc[2 * pr] = jnp.where(lane_q, qp, jnp.zeros_like(qp))
        qm_sc[2 * pr + 1] = jnp.where(lane_q, jnp.zeros_like(qp), qp)

    def block(j, diagonal):
        start = pl.multiple_of(j * tq, tq)
        tri = tri_ref[...]
        heads = range(N_HEADS)
        z = [_dot_nt(qm_sc[h], k_ref[0, pl.ds(start, tq), (h // 2) * LANES:(h // 2 + 1) * LANES])
             for h in heads]
        sp = [_softplus2(z[h]) for h in heads]
        if diagonal:
            sp = [jnp.where(causal, s, 0.0) for s in sp]
            tail = [_dot(sp[h].astype(BF16), tri) for h in heads]
        else:
            tail = [_dot(sp[h].astype(BF16), tri)
                    + jnp.concatenate([carry_sc[h]] * (tq // LANES), axis=1) for h in heads]
        att = [jnp.exp2(z[h] - tail[h]) for h in heads]
        if diagonal:
            att = [jnp.where(causal, a, 0.0) for a in att]
        mass = tail[0][:, 0:1]
        for h in heads:
            carry_sc[h] = jnp.broadcast_to(tail[h][:, 0:1], (tq, LANES))
            mass = jnp.minimum(mass, tail[h][:, 0:1])
        for pr in range(N_PAIRS):
            vblk = v_ref[0, pl.ds(start, tq), pr * LANES:(pr + 1) * LANES]
            v2 = jnp.concatenate([jnp.where(lane_q, vblk, jnp.zeros_like(vblk)),
                                  jnp.where(lane_q, jnp.zeros_like(vblk), vblk)], axis=0)
            att2 = jnp.concatenate([att[2 * pr].astype(BF16), att[2 * pr + 1].astype(BF16)], axis=1)
            pv = _dot(att2, v2)
            acc_sc[pr] = pv if diagonal else acc_sc[pr] + pv
        return jnp.min(mass)

    def body(state):
        jj, _ = state
        return jj + 1, block(i - 1 - jj, False)

    lax.while_loop(lambda s: jnp.logical_and(s[0] < i, s[1] < SKIP_BITS), body,
                   (jnp.int32(0), block(i, True)))
    for pr in range(N_PAIRS):
        y_ref[0, :, pr * LANES:(pr + 1) * LANES] = acc_sc[pr]


def _sb_prompt(qb, kb, vb, tri, tq):
    b, t, _ = qb.shape
    full = pl.BlockSpec((1, t, D_SB), lambda i, j: (i, 0, 0))
    tile = pl.BlockSpec((1, tq, D_SB), lambda i, j: (i, j, 0))
    return pl.pallas_call(
        functools.partial(_sb_prompt_kernel, tq=tq),
        grid=(b, t // tq),
        in_specs=[tile, full, full, pl.BlockSpec(tri.shape, lambda i, j: (0, 0))],
        out_specs=tile,
        out_shape=jax.ShapeDtypeStruct((b, t, D_SB), F32),
        scratch_shapes=[pltpu.VMEM((N_HEADS, tq, LANES), BF16),
                        pltpu.VMEM((N_PAIRS, tq, LANES), F32),
                        pltpu.VMEM((N_HEADS, tq, LANES), F32)],
        compiler_params=pltpu.CompilerParams(dimension_semantics=("parallel", "arbitrary"),
                                             vmem_limit_bytes=VMEM_LIMIT),
        name="sb_prompt",
    )(qb, kb, vb, tri)


def _sb_sample_kernel(q_ref, kn_ref, vn_ref, kct_ref, vct_ref, tri_ref, trin_ref, y_ref):
    t = q_ref.shape[2]
    past = kct_ref.shape[3]
    tri, trin = tri_ref[...], trin_ref[...]
    tk = tri.shape[0]
    nb = past // tk
    causal = (lax.broadcasted_iota(jnp.int32, (t, t), 1) < lax.broadcasted_iota(jnp.int32, (t, t), 0))
    heads = range(N_HEADS)
    zn = [_dot_nt(q_ref[0, h], kn_ref[0, h].astype(BF16)) for h in heads]
    zp = [_dot(q_ref[0, h], kct_ref[0, h].astype(BF16)) for h in heads]
    spn = [jnp.where(causal, _softplus2(z), 0.0) for z in zn]
    spp = [_softplus2(z) for z in zp]
    tailn = [_dot(s.astype(BF16), trin) for s in spn]
    tailp = [[_dot(s[:, c * tk:(c + 1) * tk].astype(BF16), tri) for c in range(nb)] for s in spp]
    for h in heads:
        off = jnp.broadcast_to(tailn[h][:, 0:1], (t, tk))
        for c in reversed(range(nb)):
            tailp[h][c] = tailp[h][c] + off
            off = jnp.broadcast_to(tailp[h][c][:, 0:1], (t, tk))
    for h in heads:
        attn = jnp.where(causal, jnp.exp2(zn[h] - tailn[h]), 0.0).astype(BF16)
        attp = jnp.exp2(zp[h] - jnp.concatenate(tailp[h], axis=1)).astype(BF16)
        y_ref[0, h] = (_dot(attn, vn_ref[0, h].astype(BF16))
                       + _dot_nt(attp, vct_ref[0, h].astype(BF16)))


def _sb_sample(qh, knh, vnh, kct, vct, tri, trin):
    b, h, t, d = qh.shape
    past = kct.shape[3]
    new = pl.BlockSpec((1, h, t, d), lambda i: (i, 0, 0, 0))
    old = pl.BlockSpec((1, h, d, past), lambda i: (i, 0, 0, 0))
    const = lambda a: pl.BlockSpec(a.shape, lambda i: (0, 0))
    return pl.pallas_call(
        _sb_sample_kernel,
        grid=(b,),
        in_specs=[new, new, new, old, old, const(tri), const(trin)],
        out_specs=new,
        out_shape=jax.ShapeDtypeStruct((b, h, t, d), F32),
        compiler_params=pltpu.CompilerParams(dimension_semantics=("parallel",),
                                             vmem_limit_bytes=VMEM_LIMIT),
        name="sb_sample",
    )(qh, knh, vnh, kct, vct, tri, trin)


def _merge_ffn_kernel(x_ref, ya_ref, yb_ref, gate_ref, wua_ref, wub_ref, wo_ref, g2_ref,
                      w1_ref, w2_ref, y_ref, *, ff_chunk):
    gate = gate_ref[...]
    merged = (gate[:, 0:D_MODEL] * _dot(ya_ref[...].astype(BF16), wua_ref[...])
              + gate[:, D_MODEL:2 * D_MODEL] * _dot(yb_ref[...].astype(BF16), wub_ref[...]))
    x = x_ref[...] + _dot(merged.astype(BF16), wo_ref[...])
    ms = jnp.mean(x * x, axis=-1, keepdims=True)
    xn = (x * lax.rsqrt(ms + RMS_EPS) * g2_ref[...]).astype(BF16)
    for c in range(D_FF // ff_chunk):
        sl = slice(c * ff_chunk, (c + 1) * ff_chunk)
        h = jnp.maximum(_dot(xn, w1_ref[:, sl]), 0.0)
        x = x + _dot((h * h).astype(BF16), w2_ref[sl, :])
    y_ref[...] = x


def _merge_ffn(x2d, ya, yb, gate, wua, wub, wo, g2, w1, w2, tm, ff_chunk):
    n = x2d.shape[0]
    row = lambda w: pl.BlockSpec((tm, w), lambda i: (i, 0))
    const = lambda a: pl.BlockSpec(a.shape, lambda i: (0, 0), pipeline_mode=pl.Buffered(1))
    return pl.pallas_call(
        functools.partial(_merge_ffn_kernel, ff_chunk=ff_chunk),
        grid=(n // tm,),
        in_specs=[row(D_MODEL), row(D_RWKV), row(D_SB), row(2 * D_MODEL),
                  const(wua), const(wub), const(wo), const(g2), const(w1), const(w2)],
        out_specs=row(D_MODEL),
        out_shape=jax.ShapeDtypeStruct((n, D_MODEL), F32),
        compiler_params=pltpu.CompilerParams(dimension_semantics=("parallel",),
                                             vmem_limit_bytes=VMEM_LIMIT),
        name="merge_ffn",
    )(x2d, ya, yb, gate, wua, wub, wo, g2, w1, w2)


def _pad_cols(a, width):
    return jnp.pad(a, [(0, 0)] * (a.ndim - 1) + [(0, width - a.shape[-1])])


def _pad_rwkv_cols(a):
    o1 = 3 * D_RWKV + D_DECAY_LORA
    o2 = o1 + D_AAA_LORA
    return jnp.concatenate([a[..., :3 * D_RWKV],
                            _pad_cols(a[..., 3 * D_RWKV:o1], LANES),
                            _pad_cols(a[..., o1:o2], LANES),
                            _pad_cols(a[..., o2:], 2 * LANES)], axis=-1)


def _unpad_rwkv_cols(a):
    return jnp.concatenate([a[..., :PW_OFF + D_DECAY_LORA],
                            a[..., PA_OFF:PA_OFF + D_AAA_LORA],
                            a[..., PG_OFF:PG_OFF + D_GATE_LORA]], axis=-1)


def _state_to_pairs(s):
    b = s.shape[0]
    s = s.reshape(b, N_PAIRS, 2, HEAD_DIM, HEAD_DIM)
    z = jnp.zeros_like(s[:, :, 0])
    top = jnp.concatenate([s[:, :, 0], z], axis=-1)
    bot = jnp.concatenate([z, s[:, :, 1]], axis=-1)
    return jnp.concatenate([top, bot], axis=-2)


def _pairs_to_state(sp):
    b = sp.shape[0]
    h0 = sp[:, :, :HEAD_DIM, :HEAD_DIM]
    h1 = sp[:, :, HEAD_DIM:, HEAD_DIM:]
    return jnp.stack([h0, h1], axis=2).reshape(b, N_HEADS, HEAD_DIM, HEAD_DIM)


def _to_heads(a):
    b, t, _ = a.shape
    return a.reshape(b, t, N_HEADS, HEAD_DIM).transpose(0, 2, 1, 3)


def _tri_ge(n):
    i = jnp.arange(n)
    return (i[:, None] >= i[None, :]).astype(BF16)


def _layer(x, shift_prev, s0, kt_past, vt_past, wts, consts):
    bsz, t, _ = x.shape
    tt = min(t, RWKV_ROWS)
    bb = RWKV_ROWS // tt
    n = bsz * t
    x2d = x.reshape(n, D_MODEL)
    flat = t % TILE != 0
    pr3, qb, kt, vt, kb, vb, gate = _inproj(x2d[None] if flat else x, wts["g1"], wts["w_all"],
                                            wts["gq"], wts["gk"], consts["bd"], TILE)
    if flat:
        pr3, qb = pr3.reshape(bsz, t, D_RWKV_PAD), qb.reshape(bsz, t, D_SB)
        unflat = lambda a: a.reshape(N_HEADS, HEAD_DIM, bsz, t).transpose(2, 0, 3, 1)
        k_new, v_new = unflat(kt), unflat(vt)
    else:
        k_new, v_new = jnp.swapaxes(kt, 2, 3), jnp.swapaxes(vt, 2, 3)
    gate = gate.reshape(n, 2 * D_MODEL)
    shift_new = _unpad_rwkv_cols(pr3[:, t - 1:t, :])
    ya, s_pairs = _rwkv(pr3, _pad_rwkv_cols(shift_prev), _state_to_pairs(s0.astype(F32)),
                        wts["rwkv"], consts["bd"], consts["tri_chunks"], bb, tt)
    if kt_past is None:
        yb = _sb_prompt(qb, kb, vb, consts["tri"], TILE)
    else:
        ybh = _sb_sample(_to_heads(qb), k_new, v_new, kt_past, vt_past, consts["tri"],
                         consts["tri_new"])
        yb = ybh.transpose(0, 2, 1, 3).reshape(bsz, t, D_SB)
    y = _merge_ffn(x2d, ya.reshape(n, D_RWKV), yb.reshape(n, D_SB), gate, wts["wua"], wts["wub"],
                   wts["wo"], wts["g2"], wts["w1"], wts["w2"], TILE, FF_CHUNK)
    return y.reshape(bsz, t, D_MODEL), shift_new, _pairs_to_state(s_pairs), k_new, v_new


def _prep_weights(g_norm1, w_in, rwkv_mu, rwkv_w0, rwkv_w2, rwkv_a0, rwkv_a2, rwkv_g2, rwkv_k_k,
                  rwkv_k_a, rwkv_r_k, rwkv_lnx_w, rwkv_lnx_b, sb_q_norm_g, sb_k_norm_g, w_up_a,
                  w_up_b, w_out, g_norm2, w_ff1, w_ff2):
    row = lambda a: a.reshape(1, -1).astype(F32)
    w_all = jnp.concatenate([_pad_rwkv_cols(w_in[:, :D_RWKV_IN]), w_in[:, D_RWKV_IN:]], axis=1)
    pad_rows = lambda a, rows: jnp.pad(a, ((0, rows - a.shape[0]), (0, 0))).astype(BF16)
    rwkv = (row(_pad_rwkv_cols(rwkv_mu)), row(rwkv_w0), pad_rows(rwkv_w2, LANES), row(rwkv_a0),
            pad_rows(rwkv_a2, LANES), pad_rows(rwkv_g2, 2 * LANES), row(rwkv_k_k), row(rwkv_k_a),
            row(rwkv_r_k), row(rwkv_lnx_w), row(rwkv_lnx_b))
    return {
        "g1": row(g_norm1), "w_all": w_all.astype(BF16),
        "gq": row(jnp.tile(sb_q_norm_g, N_HEADS)), "gk": row(jnp.tile(sb_k_norm_g, N_HEADS)),
        "rwkv": rwkv,
        "wua": w_up_a.astype(BF16), "wub": w_up_b.astype(BF16), "wo": w_out.astype(BF16),
        "g2": row(g_norm2), "w1": w_ff1.astype(BF16), "w2": w_ff2.astype(BF16),
    }


def _constants(t_new):
    hd = jnp.arange(2 * LANES) // HEAD_DIM
    ch = jnp.arange(RWKV_ROWS) // CHUNK
    return {
        "bd": (hd[:, None] == hd[None, :]).astype(BF16),
        "tri_chunks": _tri_ge(RWKV_ROWS) * (ch[:, None] == ch[None, :]).astype(BF16),
        "tri": _tri_ge(TILE),
        "tri_new": _tri_ge(t_new),
    }


def kernel(x_prompt, x_sample, cache_sb_k, cache_sb_v, state_rwkv_wkv, state_rwkv_shift, g_norm1, w_in, rwkv_mu, rwkv_w0, rwkv_w2, rwkv_a0, rwkv_a2, rwkv_g2, rwkv_k_k, rwkv_k_a, rwkv_r_k, rwkv_lnx_w, rwkv_lnx_b, sb_q_norm_g, sb_k_norm_g, w_up_a, w_up_b, w_out, g_norm2, w_ff1, w_ff2):
    depth = w_in.shape[0]
    assert depth == 1
    layer_w = (g_norm1, w_in, rwkv_mu, rwkv_w0, rwkv_w2, rwkv_a0, rwkv_a2, rwkv_g2, rwkv_k_k,
               rwkv_k_a, rwkv_r_k, rwkv_lnx_w, rwkv_lnx_b, sb_q_norm_g, sb_k_norm_g, w_up_a,
               w_up_b, w_out, g_norm2, w_ff1, w_ff2)
    wts = _prep_weights(*(a[0] for a in layer_w))
    consts = _constants(x_sample.shape[1])
    bp = x_prompt.shape[0]
    s0_p = jnp.zeros((bp, N_HEADS, HEAD_DIM, HEAD_DIM), F32)
    shift0_p = jnp.zeros((bp, 1, D_RWKV_IN), x_prompt.dtype)
    y_p, sh_p, s_p, k_p, v_p = _layer(x_prompt, shift0_p, s0_p, None, None, wts, consts)
    y_s, sh_s, s_s, k_s, v_s = _layer(x_sample, state_rwkv_shift[0], state_rwkv_wkv[0],
                                      jnp.swapaxes(cache_sb_k[0], 2, 3),
                                      jnp.swapaxes(cache_sb_v[0], 2, 3), wts, consts)
    return (y_p, y_s, k_p[None], v_p[None], s_p[None], sh_p[None],
            k_s[None], v_s[None], s_s[None], sh_s[None])
```

```python
import functools

import jax
import jax.numpy as jnp
from jax import lax
from jax.experimental import pallas as pl
from jax.experimental.pallas import tpu as pltpu

F32 = jnp.float32
BF16 = jnp.bfloat16

D_MODEL = 1024
HEAD_DIM = 64
D_RWKV = 512
D_SB = 512
N_HEADS = 8
N_PAIRS = 4
LANES = 128
D_DECAY_LORA = 64
D_AAA_LORA = 64
D_GATE_LORA = 160
D_RWKV_IN = 3 * D_RWKV + D_DECAY_LORA + D_AAA_LORA + D_GATE_LORA
D_FF = 4 * D_MODEL
CHUNK = 64
RWKV_ROWS = 256
TILE = 256
DENSE_TILE = 512
FF_CHUNK = 1024
SB_SCALE = HEAD_DIM ** -0.5
LOG2E = 1.4426950408889634
Q_SCALE = SB_SCALE * LOG2E
SKIP_BITS = 150.0
RMS_EPS = 1e-6
GN_EPS = 64e-5
L2_EPS = 1e-24

PW_OFF = 3 * D_RWKV
PA_OFF = PW_OFF + LANES
PG_OFF = PA_OFF + LANES
D_RWKV_PAD = PG_OFF + 2 * LANES
SB_OFF = D_RWKV_PAD
GATE_OFF = SB_OFF + 3 * D_SB
D_IN_PAD = GATE_OFF + 2 * D_MODEL

VMEM_LIMIT = 56 * 1024 * 1024


def _dot(a, b):
    return jnp.dot(a, b, preferred_element_type=F32)


def _dot_nt(a, b):
    return lax.dot_general(a, b, (((1,), (1,)), ((), ())), preferred_element_type=F32)


def _split(x):
    hi = x.astype(BF16)
    lo = (x - hi.astype(F32)).astype(BF16)
    return hi, lo


def _dot_lh(m, x):
    hi, lo = _split(x)
    return _dot(m, hi) + _dot(m, lo)


def _head_sum(x, bd):
    xb = x.astype(BF16)
    w = bd.shape[0]
    return jnp.concatenate([_dot(xb[:, c * w:(c + 1) * w], bd) for c in range(x.shape[1] // w)],
                           axis=1)


def _mm(a, b):
    return _dot(a.astype(BF16), b.astype(BF16))


def _mm_nt(a, b):
    return _dot_nt(a.astype(BF16), b.astype(BF16))


def _softplus(z):
    return jnp.maximum(z, 0.0) + jnp.log(1.0 + jnp.exp(-jnp.abs(z)))


def _softplus2(z):
    neg_abs = lax.bitcast_convert_type(lax.bitcast_convert_type(z, jnp.uint32) | jnp.uint32(1 << 31), F32)
    return jnp.maximum(z, 0.0) + jnp.log2(1.0 + jnp.exp2(neg_abs))


def _sigmoid(z):
    return 0.5 * jnp.tanh(0.5 * z) + 0.5


def _inproj_kernel(x_ref, g1_ref, w_ref, gq_ref, gk_ref, bd_ref,
                   pr_ref, q_ref, kh_ref, vh_ref, kb_ref, vb_ref, gate_ref):
    x = x_ref[0]
    ms = jnp.mean(x * x, axis=-1, keepdims=True)
    xn = (x * lax.rsqrt(ms + RMS_EPS) * g1_ref[...]).astype(BF16)
    pr_ref[0] = _dot(xn, w_ref[:, 0:D_RWKV_PAD])
    bd = bd_ref[...]

    def head_norm(u, g):
        ss = _head_sum(u * u, bd) * (1.0 / HEAD_DIM)
        return u * lax.rsqrt(ss + RMS_EPS) * g

    q = head_norm(_dot(xn, w_ref[:, SB_OFF:SB_OFF + D_SB]), gq_ref[...])
    k = head_norm(_dot(xn, w_ref[:, SB_OFF + D_SB:SB_OFF + 2 * D_SB]), gk_ref[...])
    v = _dot(xn, w_ref[:, SB_OFF + 2 * D_SB:SB_OFF + 3 * D_SB])
    q_ref[0] = (q * Q_SCALE).astype(BF16)
    kb_ref[0] = k.astype(BF16)
    vb_ref[0] = v.astype(BF16)
    kh_ref[0] = k.T.reshape(N_HEADS, HEAD_DIM, k.shape[0])
    vh_ref[0] = v.T.reshape(N_HEADS, HEAD_DIM, v.shape[0])
    gate_ref[0] = _sigmoid(_dot(xn, w_ref[:, GATE_OFF:GATE_OFF + 2 * D_MODEL]))


def _inproj(x, g1, w_all, gq, gk, bd, tm):
    b, t, _ = x.shape
    row = lambda w: pl.BlockSpec((1, tm, w), lambda i, j: (i, j, 0))
    heads = pl.BlockSpec((1, N_HEADS, HEAD_DIM, tm), lambda i, j: (i, 0, 0, j))
    const = lambda a: pl.BlockSpec(a.shape, lambda i, j: (0, 0), pipeline_mode=pl.Buffered(1))
    return pl.pallas_call(
        _inproj_kernel,
        grid=(b, t // tm),
        in_specs=[row(D_MODEL), const(g1), const(w_all), const(gq), const(gk), const(bd)],
        out_specs=[row(D_RWKV_PAD), row(D_SB), heads, heads, row(D_SB), row(D_SB),
                   row(2 * D_MODEL)],
        out_shape=[jax.ShapeDtypeStruct((b, t, D_RWKV_PAD), F32),
                   jax.ShapeDtypeStruct((b, t, D_SB), BF16),
                   jax.ShapeDtypeStruct((b, N_HEADS, HEAD_DIM, t), F32),
                   jax.ShapeDtypeStruct((b, N_HEADS, HEAD_DIM, t), F32),
                   jax.ShapeDtypeStruct((b, t, D_SB), BF16),
                   jax.ShapeDtypeStruct((b, t, D_SB), BF16),
                   jax.ShapeDtypeStruct((b, t, 2 * D_MODEL), F32)],
        compiler_params=pltpu.CompilerParams(dimension_semantics=("parallel", "parallel"),
                                             vmem_limit_bytes=VMEM_LIMIT),
        name="inproj",
    )(x, g1, w_all, gq, gk, bd)


def _rwkv_kernel(p_ref, shift_ref, s0_ref, mu_ref, w0_ref, w2_ref, a0_ref, a2_ref, g2_ref,
                 kk_ref, ka_ref, rk_ref, lw_ref, lb_ref, bd_ref, tri_ref,
                 y_ref, sout_ref, carry_sc, s_sc, *, bb, tt):
    c = pl.program_id(1)
    rows = bb * tt

    @pl.when(c == 0)
    def _():
        carry_sc[...] = shift_ref[...]
        s_sc[...] = s0_ref[...]

    p = p_ref[...].reshape(rows, D_RWKV_PAD)
    row = lax.broadcasted_iota(jnp.int32, p.shape, 0)
    prev = pltpu.roll(p, 1, 0)
    for b_ in range(bb):
        prev = jnp.where(row == b_ * tt, carry_sc[b_], prev)
        carry_sc[b_] = p[(b_ + 1) * tt - 1:(b_ + 1) * tt, :]
    pm = p + (prev - p) * mu_ref[...]
    r = pm[:, 0:D_RWKV]
    k = pm[:, D_RWKV:2 * D_RWKV]
    v = pm[:, 2 * D_RWKV:3 * D_RWKV]
    wl = pm[:, PW_OFF:PW_OFF + LANES]
    al = pm[:, PA_OFF:PA_OFF + LANES]
    gl = pm[:, PG_OFF:PG_OFF + 2 * LANES]

    bd = bd_ref[...]
    w = -_softplus(-(w0_ref[...] + _mm(jnp.tanh(wl), w2_ref[...]))) - 0.5
    ld = -jnp.exp(w)
    a = _sigmoid(a0_ref[...] + _mm(al, a2_ref[...]))
    g = _mm(_sigmoid(gl), g2_ref[...])
    kk = k * kk_ref[...]
    kk = kk * lax.rsqrt(jnp.maximum(_head_sum(kk * kk, bd), L2_EPS))
    k = k * (1.0 + (a - 1.0) * ka_ref[...])
    bonus = _head_sum(r * k * rk_ref[...], bd) * v

    n_ch = tt // CHUNK
    chunks = [(b_, ch) for b_ in range(bb) for ch in range(n_ch)]
    rows_of = lambda b_, ch: slice(b_ * tt + ch * CHUNK, b_ * tt + (ch + 1) * CHUNK)
    cl = _dot_lh(tri_ref[...], ld)
    cl_end = jnp.concatenate(
        [jnp.broadcast_to(cl[rows_of(*c).stop - 1:rows_of(*c).stop, :], (CHUNK, D_RWKV))
         for c in chunks], axis=0)
    w_inv = jnp.exp(-cl)
    w_end = jnp.exp(cl_end - cl)
    at = -kk * jnp.exp(cl - ld)
    bt = kk * a * w_inv
    kt = k * w_inv
    rt = r * jnp.exp(cl)
    be = kk * a * w_end
    ke = k * w_end
    wc = jnp.exp(cl_end)

    lane = lax.broadcasted_iota(jnp.int32, (CHUNK, LANES), 1)
    m0 = lane < HEAD_DIM
    ri = lax.broadcasted_iota(jnp.int32, (LANES, LANES), 0)
    ci = lax.broadcasted_iota(jnp.int32, (LANES, LANES), 1)
    same = (ri // CHUNK) == (ci // CHUNK)
    strict = same & ((ri % CHUNK) > (ci % CHUNK))
    incl = same & ((ri % CHUNK) >= (ci % CHUNK))
    eye = (ri == ci).astype(F32)

    def stack(u):
        return jnp.concatenate([jnp.where(m0, u, 0.0), jnp.where(m0, 0.0, u)], axis=0)

    keys = [(b_, ch, pr) for (b_, ch) in chunks for pr in range(N_PAIRS)]
    blk = lambda x, key: x[rows_of(key[0], key[1]), key[2] * LANES:(key[2] + 1) * LANES]
    la = {q: stack(blk(at, q)) for q in keys}
    lr = {q: stack(blk(rt, q)) for q in keys}
    vst = {q: stack(blk(v, q)) for q in keys}
    twice = lambda u: jnp.concatenate([u, u], axis=0)
    g4 = {q: _mm_nt(jnp.concatenate([la[q], lr[q]], axis=0),
                    jnp.concatenate([twice(blk(bt, q)), twice(blk(kt, q))], axis=0)) for q in keys}
    a_ab = {q: jnp.where(strict, g4[q][:LANES, :LANES], 0.0) for q in keys}
    a_ak = {q: jnp.where(strict, g4[q][:LANES, LANES:], 0.0) for q in keys}
    a_rb = {q: jnp.where(incl, g4[q][LANES:, :LANES], 0.0) for q in keys}
    a_rk = {q: jnp.where(incl, g4[q][LANES:, LANES:], 0.0) for q in keys}
    tinv = {q: eye + a_ab[q] for q in keys}
    apow = {q: _mm(a_ab[q], a_ab[q]) for q in keys}
    for _ in range(4):
        nxt = {q: _mm(apow[q], jnp.concatenate([apow[q], tinv[q]], axis=1)) for q in keys}
        apow = {q: nxt[q][:, :LANES] for q in keys}
        tinv = {q: tinv[q] + nxt[q][:, LANES:] for q in keys}
    tinv = {q: tinv[q] + _mm(apow[q], tinv[q]) for q in keys}
    akv = {q: _mm(jnp.concatenate([a_ak[q], a_rk[q]], axis=0), vst[q]) for q in keys}
    tlav = {q: _mm(tinv[q], jnp.concatenate([la[q], akv[q][:LANES]], axis=1)) for q in keys}
    rlon = {q: _mm(a_rb[q], tlav[q]) for q in keys}
    rl = {q: lr[q] + rlon[q][:, :LANES] for q in keys}
    on = {q: rlon[q][:, LANES:] + akv[q][LANES:] for q in keys}
    mn = {q: _mm(tlav[q].T, stack(blk(be, q))) for q in keys}
    nc = {q: mn[q][LANES:] + _mm(vst[q].T, stack(blk(ke, q))) for q in keys}

    out_rows = []
    for b_ in range(bb):
        state = [s_sc[b_, pr] for pr in range(N_PAIRS)]
        for ch in range(n_ch):
            outs = []
            for pr in range(N_PAIRS):
                q = (b_, ch, pr)
                s_b = state[pr].astype(BF16)
                o_st = _dot_nt(rl[q].astype(BF16), s_b) + on[q]
                outs.append(o_st[0:CHUNK] + o_st[CHUNK:2 * CHUNK])
                state[pr] = (state[pr] * blk(wc, q)[0:1, :] + _dot(s_b, mn[q][:LANES].astype(BF16))
                             + nc[q])
            out_rows.append(jnp.concatenate(outs, axis=1))
        for pr in range(N_PAIRS):
            s_sc[b_, pr] = state[pr]
            sout_ref[b_, pr] = state[pr]
    o = jnp.concatenate(out_rows, axis=0)

    mean = _head_sum(o, bd) * (1.0 / HEAD_DIM)
    oc = o - mean
    var = _head_sum(oc * oc, bd) * (1.0 / HEAD_DIM)
    o = oc * lax.rsqrt(var + GN_EPS)
    y_ref[...] = ((o * lw_ref[...] + lb_ref[...] + bonus) * g).reshape(bb, tt, D_RWKV)


def _rwkv(p3d, shift_pad, s0_bd, params, bd, tri, bb, tt):
    b, t, _ = p3d.shape
    const = lambda a: pl.BlockSpec(a.shape, lambda i, j: (0,) * a.ndim)
    state = pl.BlockSpec((bb, N_PAIRS, LANES, LANES), lambda i, j: (i, 0, 0, 0))
    return pl.pallas_call(
        functools.partial(_rwkv_kernel, bb=bb, tt=tt),
        grid=(b // bb, t // tt),
        in_specs=[pl.BlockSpec((bb, tt, D_RWKV_PAD), lambda i, j: (i, j, 0)),
                  pl.BlockSpec((bb, 1, D_RWKV_PAD), lambda i, j: (i, 0, 0)), state]
                 + [const(a) for a in params] + [const(bd), const(tri)],
        out_specs=[pl.BlockSpec((bb, tt, D_RWKV), lambda i, j: (i, j, 0)), state],
        out_shape=[jax.ShapeDtypeStruct((b, t, D_RWKV), F32),
                   jax.ShapeDtypeStruct((b, N_PAIRS, LANES, LANES), F32)],
        scratch_shapes=[pltpu.VMEM((bb, 1, D_RWKV_PAD), F32),
                        pltpu.VMEM((bb, N_PAIRS, LANES, LANES), F32)],
        compiler_params=pltpu.CompilerParams(dimension_semantics=("parallel", "arbitrary"),
                                             vmem_limit_bytes=VMEM_LIMIT),
        name="rwkv",
    )(p3d, shift_pad, s0_bd, *params, bd, tri)


def _sb_prompt_kernel(q_ref, k_ref, v_ref, tri_ref, y_ref, qm_sc, acc_sc, carry_sc, *, tq):
    i = pl.program_id(1)
    lane_q = lax.broadcasted_iota(jnp.int32, (tq, LANES), 1) < HEAD_DIM
    causal = (lax.broadcasted_iota(jnp.int32, (tq, tq), 1)
              < lax.broadcasted_iota(jnp.int32, (tq, tq), 0))
    for pr in range(N_PAIRS):
        qp = q_ref[0, :, pr * LANES:(pr + 1) * LANES]
        qm_sc[2 * pr] = jnp.where(lane_q, qp, jnp.zeros_like(qp))
        qm_sc[2 * pr + 1] = jnp.where(lane_q, jnp.zeros_like(qp), qp)

    def block(j, diagonal):
        start = pl.multiple_of(j * tq, tq)
        tri = tri_ref[...]
        heads = range(N_HEADS)
        z = [_dot_nt(qm_sc[h], k_ref[0, pl.ds(start, tq), (h // 2) * LANES:(h // 2 + 1) * LANES])
             for h in heads]
        sp = [_softplus2(z[h]) for h in heads]
        if diagonal:
            sp = [jnp.where(causal, s, 0.0) for s in sp]
            tail = [_dot(sp[h].astype(BF16), tri) for h in heads]
        else:
            tail = [_dot(sp[h].astype(BF16), tri)
                    + jnp.concatenate([carry_sc[h]] * (tq // LANES), axis=1) for h in heads]
        att = [jnp.exp2(z[h] - tail[h]) for h in heads]
        if diagonal:
            att = [jnp.where(causal, a, 0.0) for a in att]
        mass = tail[0][:, 0:1]
        for h in heads:
            carry_sc[h] = jnp.broadcast_to(tail[h][:, 0:1], (tq, LANES))
            mass = jnp.minimum(mass, tail[h][:, 0:1])
        for pr in range(N_PAIRS):
            vblk = v_ref[0, pl.ds(start, tq), pr * LANES:(pr + 1) * LANES]
            v2 = jnp.concatenate([jnp.where(lane_q, vblk, jnp.zeros_like(vblk)),
                                  jnp.where(lane_q, jnp.zeros_like(vblk), vblk)], axis=0)
            att2 = jnp.concatenate([att[2 * pr].astype(BF16), att[2 * pr + 1].astype(BF16)], axis=1)
            pv = _dot(att2, v2)
            acc_sc[pr] = pv if diagonal else acc_sc[pr] + pv
        return jnp.min(mass)

    def body(state):
        jj, _ = state
        return jj + 1, block(i - 1 - jj, False)

    lax.while_loop(lambda s: jnp.logical_and(s[0] < i, s[1] < SKIP_BITS), body,
                   (jnp.int32(0), block(i, True)))
    for pr in range(N_PAIRS):
        y_ref[0, :, pr * LANES:(pr + 1) * LANES] = acc_sc[pr]


def _sb_prompt(qb, kb, vb, tri, tq):
    b, t, _ = qb.shape
    full = pl.BlockSpec((1, t, D_SB), lambda i, j: (i, 0, 0))
    tile = pl.BlockSpec((1, tq, D_SB), lambda i, j: (i, j, 0))
    return pl.pallas_call(
        functools.partial(_sb_prompt_kernel, tq=tq),
        grid=(b, t // tq),
        in_specs=[tile, full, full, pl.BlockSpec(tri.shape, lambda i, j: (0, 0))],
        out_specs=tile,
        out_shape=jax.ShapeDtypeStruct((b, t, D_SB), F32),
        scratch_shapes=[pltpu.VMEM((N_HEADS, tq, LANES), BF16),
                        pltpu.VMEM((N_PAIRS, tq, LANES), F32),
                        pltpu.VMEM((N_HEADS, tq, LANES), F32)],
        compiler_params=pltpu.CompilerParams(dimension_semantics=("parallel", "arbitrary"),
                                             vmem_limit_bytes=VMEM_LIMIT),
        name="sb_prompt",
    )(qb, kb, vb, tri)


def _sb_sample_kernel(q_ref, kn_ref, vn_ref, kct_ref, vct_ref, tri_ref, trin_ref, y_ref):
    t = q_ref.shape[1]
    past = kct_ref.shape[3]
    tri, trin = tri_ref[...], trin_ref[...]
    tk = tri.shape[0]
    nb = past // tk
    causal = (lax.broadcasted_iota(jnp.int32, (t, t), 1) < lax.broadcasted_iota(jnp.int32, (t, t), 0))
    lane_lo = lax.broadcasted_iota(jnp.int32, (t, LANES), 1) < HEAD_DIM
    row_lo = lax.broadcasted_iota(jnp.int32, (LANES, past), 0) < HEAD_DIM
    heads = range(N_HEADS)
    pair = lambda ref, pr: ref[0, :, pr * LANES:(pr + 1) * LANES]
    own = lambda h, u, mask: jnp.where(mask if h % 2 == 0 else jnp.logical_not(mask), u,
                                       jnp.zeros_like(u))
    cache = lambda ref, pr: ref[0, 2 * pr:2 * pr + 2].reshape(LANES, past).astype(BF16)
    kct = [cache(kct_ref, pr) for pr in range(N_PAIRS)]
    vct = [cache(vct_ref, pr) for pr in range(N_PAIRS)]
    qm = [own(h, pair(q_ref, h // 2), lane_lo) for h in heads]
    zn = [_dot_nt(qm[h], pair(kn_ref, h // 2)) for h in heads]
    zp = [_dot(qm[h], kct[h // 2]) for h in heads]
    spn = [jnp.where(causal, _softplus2(z), 0.0) for z in zn]
    spp = [_softplus2(z) for z in zp]
    tailn = [_dot(s.astype(BF16), trin) for s in spn]
    tailp = [[_dot(s[:, c * tk:(c + 1) * tk].astype(BF16), tri) for c in range(nb)] for s in spp]
    for h in heads:
        off = jnp.broadcast_to(tailn[h][:, 0:1], (t, tk))
        for c in reversed(range(nb)):
            tailp[h][c] = tailp[h][c] + off
            off = jnp.broadcast_to(tailp[h][c][:, 0:1], (t, tk))
    out = [None] * N_PAIRS
    for h in heads:
        attn = jnp.where(causal, jnp.exp2(zn[h] - tailn[h]), 0.0).astype(BF16)
        attp = jnp.exp2(zp[h] - jnp.concatenate(tailp[h], axis=1)).astype(BF16)
        yh = (_dot(attn, own(h, pair(vn_ref, h // 2), lane_lo))
              + _dot_nt(attp, own(h, vct[h // 2], row_lo)))
        out[h // 2] = yh if out[h // 2] is None else out[h // 2] + yh
    for pr in range(N_PAIRS):
        y_ref[0, :, pr * LANES:(pr + 1) * LANES] = out[pr]


def _sb_sample(qb, kb, vb, kct, vct, tri, trin):
    b, t, _ = qb.shape
    past = kct.shape[3]
    new = pl.BlockSpec((1, t, D_SB), lambda i: (i, 0, 0))
    old = pl.BlockSpec((1, N_HEADS, HEAD_DIM, past), lambda i: (i, 0, 0, 0))
    const = lambda a: pl.BlockSpec(a.shape, lambda i: (0, 0))
    return pl.pallas_call(
        _sb_sample_kernel,
        grid=(b,),
        in_specs=[new, new, new, old, old, const(tri), const(trin)],
        out_specs=new,
        out_shape=jax.ShapeDtypeStruct((b, t, D_SB), F32),
        compiler_params=pltpu.CompilerParams(dimension_semantics=("parallel",),
                                             vmem_limit_bytes=VMEM_LIMIT),
        name="sb_sample",
    )(qb, kb, vb, kct, vct, tri, trin)


def _merge_ffn_kernel(x_ref, ya_ref, yb_ref, gate_ref, wua_ref, wub_ref, wo_ref, g2_ref,
                      w1_ref, w2_ref, y_ref, *, ff_chunk):
    gate = gate_ref[...]
    merged = (gate[:, 0:D_MODEL] * _dot(ya_ref[...].astype(BF16), wua_ref[...])
              + gate[:, D_MODEL:2 * D_MODEL] * _dot(yb_ref[...].astype(BF16), wub_ref[...]))
    x = x_ref[...] + _dot(merged.astype(BF16), wo_ref[...])
    ms = jnp.mean(x * x, axis=-1, keepdims=True)
    xn = (x * lax.rsqrt(ms + RMS_EPS) * g2_ref[...]).astype(BF16)
    for c in range(D_FF // ff_chunk):
        sl = slice(c * ff_chunk, (c + 1) * ff_chunk)
        h = jnp.maximum(_dot(xn, w1_ref[:, sl]), 0.0)
        x = x + _dot((h * h).astype(BF16), w2_ref[sl, :])
    y_ref[...] = x


def _merge_ffn(x2d, ya, yb, gate, wua, wub, wo, g2, w1, w2, tm, ff_chunk):
    n = x2d.shape[0]
    row = lambda w: pl.BlockSpec((tm, w), lambda i: (i, 0))
    const = lambda a: pl.BlockSpec(a.shape, lambda i: (0, 0), pipeline_mode=pl.Buffered(1))
    return pl.pallas_call(
        functools.partial(_merge_ffn_kernel, ff_chunk=ff_chunk),
        grid=(n // tm,),
        in_specs=[row(D_MODEL), row(D_RWKV), row(D_SB), row(2 * D_MODEL),
                  const(wua), const(wub), const(wo), const(g2), const(w1), const(w2)],
        out_specs=row(D_MODEL),
        out_shape=jax.ShapeDtypeStruct((n, D_MODEL), F32),
        compiler_params=pltpu.CompilerParams(dimension_semantics=("parallel",),
                                             vmem_limit_bytes=VMEM_LIMIT),
        name="merge_ffn",
    )(x2d, ya, yb, gate, wua, wub, wo, g2, w1, w2)


def _pad_cols(a, width):
    return jnp.pad(a, [(0, 0)] * (a.ndim - 1) + [(0, width - a.shape[-1])])


def _pad_rwkv_cols(a):
    o1 = 3 * D_RWKV + D_DECAY_LORA
    o2 = o1 + D_AAA_LORA
    return jnp.concatenate([a[..., :3 * D_RWKV],
                            _pad_cols(a[..., 3 * D_RWKV:o1], LANES),
                            _pad_cols(a[..., o1:o2], LANES),
                            _pad_cols(a[..., o2:], 2 * LANES)], axis=-1)


def _unpad_rwkv_cols(a):
    return jnp.concatenate([a[..., :PW_OFF + D_DECAY_LORA],
                            a[..., PA_OFF:PA_OFF + D_AAA_LORA],
                            a[..., PG_OFF:PG_OFF + D_GATE_LORA]], axis=-1)


def _state_to_pairs(s):
    b = s.shape[0]
    s = s.reshape(b, N_PAIRS, 2, HEAD_DIM, HEAD_DIM)
    z = jnp.zeros_like(s[:, :, 0])
    top = jnp.concatenate([s[:, :, 0], z], axis=-1)
    bot = jnp.concatenate([z, s[:, :, 1]], axis=-1)
    return jnp.concatenate([top, bot], axis=-2)


def _pairs_to_state(sp):
    b = sp.shape[0]
    h0 = sp[:, :, :HEAD_DIM, :HEAD_DIM]
    h1 = sp[:, :, HEAD_DIM:, HEAD_DIM:]
    return jnp.stack([h0, h1], axis=2).reshape(b, N_HEADS, HEAD_DIM, HEAD_DIM)


def _tri_ge(n):
    i = jnp.arange(n)
    return (i[:, None] >= i[None, :]).astype(BF16)


def _layer(x, shift_prev, s0, kt_past, vt_past, wts, consts):
    bsz, t, _ = x.shape
    tt = min(t, RWKV_ROWS)
    bb = RWKV_ROWS // tt
    n = bsz * t
    x2d = x.reshape(n, D_MODEL)
    flat = t % DENSE_TILE != 0
    pr3, qb, kt, vt, kb, vb, gate = _inproj(x2d[None] if flat else x, wts["g1"], wts["w_all"],
                                            wts["gq"], wts["gk"], consts["bd"], DENSE_TILE)
    if flat:
        pr3 = pr3.reshape(bsz, t, D_RWKV_PAD)
        qb, kb, vb = (a.reshape(bsz, t, D_SB) for a in (qb, kb, vb))
        unflat = lambda a: a.reshape(N_HEADS, HEAD_DIM, bsz, t).transpose(2, 0, 3, 1)
        k_new, v_new = unflat(kt), unflat(vt)
    else:
        k_new, v_new = jnp.swapaxes(kt, 2, 3), jnp.swapaxes(vt, 2, 3)
    gate = gate.reshape(n, 2 * D_MODEL)
    shift_new = _unpad_rwkv_cols(pr3[:, t - 1:t, :])
    ya, s_pairs = _rwkv(pr3, _pad_rwkv_cols(shift_prev), _state_to_pairs(s0.astype(F32)),
                        wts["rwkv"], consts["bd"], consts["tri_chunks"], bb, tt)
    if kt_past is None:
        yb = _sb_prompt(qb, kb, vb, consts["tri"], TILE)
    else:
        yb = _sb_sample(qb, kb, vb, kt_past, vt_past, consts["tri"], consts["tri_new"])
    y = _merge_ffn(x2d, ya.reshape(n, D_RWKV), yb.reshape(n, D_SB), gate, wts["wua"], wts["wub"],
                   wts["wo"], wts["g2"], wts["w1"], wts["w2"], DENSE_TILE, FF_CHUNK)
    return y.reshape(bsz, t, D_MODEL), shift_new, _pairs_to_state(s_pairs), k_new, v_new


def _prep_weights(g_norm1, w_in, rwkv_mu, rwkv_w0, rwkv_w2, rwkv_a0, rwkv_a2, rwkv_g2, rwkv_k_k,
                  rwkv_k_a, rwkv_r_k, rwkv_lnx_w, rwkv_lnx_b, sb_q_norm_g, sb_k_norm_g, w_up_a,
                  w_up_b, w_out, g_norm2, w_ff1, w_ff2):
    row = lambda a: a.reshape(1, -1).astype(F32)
    w_all = jnp.concatenate([_pad_rwkv_cols(w_in[:, :D_RWKV_IN]), w_in[:, D_RWKV_IN:]], axis=1)
    pad_rows = lambda a, rows: jnp.pad(a, ((0, rows - a.shape[0]), (0, 0))).astype(BF16)
    rwkv = (row(_pad_rwkv_cols(rwkv_mu)), row(rwkv_w0), pad_rows(rwkv_w2, LANES), row(rwkv_a0),
            pad_rows(rwkv_a2, LANES), pad_rows(rwkv_g2, 2 * LANES), row(rwkv_k_k), row(rwkv_k_a),
            row(rwkv_r_k), row(rwkv_lnx_w), row(rwkv_lnx_b))
    return {
        "g1": row(g_norm1), "w_all": w_all.astype(BF16),
        "gq": row(jnp.tile(sb_q_norm_g, N_HEADS)), "gk": row(jnp.tile(sb_k_norm_g, N_HEADS)),
        "rwkv": rwkv,
        "wua": w_up_a.astype(BF16), "wub": w_up_b.astype(BF16), "wo": w_out.astype(BF16),
        "g2": row(g_norm2), "w1": w_ff1.astype(BF16), "w2": w_ff2.astype(BF16),
    }


def _constants(t_new):
    hd = jnp.arange(2 * LANES) // HEAD_DIM
    ch = jnp.arange(RWKV_ROWS) // CHUNK
    return {
        "bd": (hd[:, None] == hd[None, :]).astype(BF16),
        "tri_chunks": _tri_ge(RWKV_ROWS) * (ch[:, None] == ch[None, :]).astype(BF16),
        "tri": _tri_ge(TILE),
        "tri_new": _tri_ge(t_new),
    }


def kernel(x_prompt, x_sample, cache_sb_k, cache_sb_v, state_rwkv_wkv, state_rwkv_shift, g_norm1, w_in, rwkv_mu, rwkv_w0, rwkv_w2, rwkv_a0, rwkv_a2, rwkv_g2, rwkv_k_k, rwkv_k_a, rwkv_r_k, rwkv_lnx_w, rwkv_lnx_b, sb_q_norm_g, sb_k_norm_g, w_up_a, w_up_b, w_out, g_norm2, w_ff1, w_ff2):
    depth = w_in.shape[0]
    assert depth == 1
    layer_w = (g_norm1, w_in, rwkv_mu, rwkv_w0, rwkv_w2, rwkv_a0, rwkv_a2, rwkv_g2, rwkv_k_k,
               rwkv_k_a, rwkv_r_k, rwkv_lnx_w, rwkv_lnx_b, sb_q_norm_g, sb_k_norm_g, w_up_a,
               w_up_b, w_out, g_norm2, w_ff1, w_ff2)
    wts = _prep_weights(*(a[0] for a in layer_w))
    consts = _constants(x_sample.shape[1])
    bp = x_prompt.shape[0]
    s0_p = jnp.zeros((bp, N_HEADS, HEAD_DIM, HEAD_DIM), F32)
    shift0_p = jnp.zeros((bp, 1, D_RWKV_IN), x_prompt.dtype)
    y_p, sh_p, s_p, k_p, v_p = _layer(x_prompt, shift0_p, s0_p, None, None, wts, consts)
    y_s, sh_s, s_s, k_s, v_s = _layer(x_sample, state_rwkv_shift[0], state_rwkv_wkv[0],
                                      jnp.swapaxes(cache_sb_k[0], 2, 3),
                                      jnp.swapaxes(cache_sb_v[0], 2, 3), wts, consts)
    return (y_p, y_s, k_p[None], v_p[None], s_p[None], sh_p[None],
            k_s[None], v_s[None], s_s[None], sh_s[None])
```

```python
import functools

import jax
import jax.numpy as jnp
from jax import lax
from jax.experimental import pallas as pl
from jax.experimental.pallas import tpu as pltpu

F32 = jnp.float32
BF16 = jnp.bfloat16

D_MODEL = 1024
HEAD_DIM = 64
D_RWKV = 512
D_SB = 512
N_HEADS = 8
N_PAIRS = 4
LANES = 128
SUBLANES = 8
D_DECAY_LORA = 64
D_AAA_LORA = 64
D_GATE_LORA = 160
D_RWKV_IN = 3 * D_RWKV + D_DECAY_LORA + D_AAA_LORA + D_GATE_LORA
D_FF = 4 * D_MODEL
CHUNK = 64
RWKV_ROWS = 256
TILE = 256
DENSE_TILE = 512
FF_CHUNK = 1024
SB_SCALE = HEAD_DIM ** -0.5
LOG2E = 1.4426950408889634
Q_SCALE = SB_SCALE * LOG2E
SKIP_BITS = 150.0
RMS_EPS = 1e-6
GN_EPS = 64e-5
L2_EPS = 1e-24

PW_OFF = 3 * D_RWKV
PA_OFF = PW_OFF + LANES
PG_OFF = PA_OFF + LANES
D_RWKV_PAD = PG_OFF + 2 * LANES
SB_OFF = D_RWKV_PAD
GATE_OFF = SB_OFF + 3 * D_SB
D_IN_PAD = GATE_OFF + 2 * D_MODEL

VMEM_LIMIT = 56 * 1024 * 1024


def _dot(a, b):
    return jnp.dot(a, b, preferred_element_type=F32)


def _dot_nt(a, b):
    return lax.dot_general(a, b, (((1,), (1,)), ((), ())), preferred_element_type=F32)


def _split(x):
    hi = x.astype(BF16)
    lo = (x - hi.astype(F32)).astype(BF16)
    return hi, lo


def _dot_lh(m, x):
    hi, lo = _split(x)
    return _dot(m, hi) + _dot(m, lo)


def _head_sum(x, bd):
    xb = x.astype(BF16)
    w = bd.shape[0]
    return jnp.concatenate([_dot(xb[:, c * w:(c + 1) * w], bd) for c in range(x.shape[1] // w)],
                           axis=1)


def _mm(a, b):
    return _dot(a.astype(BF16), b.astype(BF16))


def _mm_nt(a, b):
    return _dot_nt(a.astype(BF16), b.astype(BF16))


def _softplus(z):
    return jnp.maximum(z, 0.0) + jnp.log(1.0 + jnp.exp(-jnp.abs(z)))


def _softplus2(z):
    return jnp.maximum(z, 0.0) + jnp.log2(1.0 + jnp.exp2(-jnp.abs(z)))


def _sigmoid(z):
    return 0.5 * jnp.tanh(0.5 * z) + 0.5


def _inproj_kernel(x_ref, g1_ref, w_ref, gq_ref, gk_ref, bd_ref,
                   pr_ref, q_ref, kh_ref, vh_ref, kb_ref, vb_ref, gate_ref):
    x = x_ref[0]
    ms = jnp.mean(x * x, axis=-1, keepdims=True)
    xn = (x * lax.rsqrt(ms + RMS_EPS) * g1_ref[...]).astype(BF16)
    pr_ref[0] = _dot(xn, w_ref[:, 0:D_RWKV_PAD])
    bd = bd_ref[...]

    def head_norm(u, g):
        ss = _head_sum(u * u, bd) * (1.0 / HEAD_DIM)
        return u * lax.rsqrt(ss + RMS_EPS) * g

    q = head_norm(_dot(xn, w_ref[:, SB_OFF:SB_OFF + D_SB]), gq_ref[...])
    k = head_norm(_dot(xn, w_ref[:, SB_OFF + D_SB:SB_OFF + 2 * D_SB]), gk_ref[...])
    v = _dot(xn, w_ref[:, SB_OFF + 2 * D_SB:SB_OFF + 3 * D_SB])
    q_ref[0] = (q * Q_SCALE).astype(BF16)
    kb_ref[0] = k.astype(BF16)
    vb_ref[0] = v.astype(BF16)
    kh_ref[0] = k.T.reshape(N_HEADS, HEAD_DIM, k.shape[0])
    vh_ref[0] = v.T.reshape(N_HEADS, HEAD_DIM, v.shape[0])
    gate_ref[0] = _sigmoid(_dot(xn, w_ref[:, GATE_OFF:GATE_OFF + 2 * D_MODEL]))


def _inproj(x, g1, w_all, gq, gk, bd, tm):
    b, t, _ = x.shape
    row = lambda w: pl.BlockSpec((1, tm, w), lambda i, j: (i, j, 0))
    heads = pl.BlockSpec((1, N_HEADS, HEAD_DIM, tm), lambda i, j: (i, 0, 0, j))
    const = lambda a: pl.BlockSpec(a.shape, lambda i, j: (0, 0), pipeline_mode=pl.Buffered(1))
    return pl.pallas_call(
        _inproj_kernel,
        grid=(b, t // tm),
        in_specs=[row(D_MODEL), const(g1), const(w_all), const(gq), const(gk), const(bd)],
        out_specs=[row(D_RWKV_PAD), row(D_SB), heads, heads, row(D_SB), row(D_SB),
                   row(2 * D_MODEL)],
        out_shape=[jax.ShapeDtypeStruct((b, t, D_RWKV_PAD), F32),
                   jax.ShapeDtypeStruct((b, t, D_SB), BF16),
                   jax.ShapeDtypeStruct((b, N_HEADS, HEAD_DIM, t), F32),
                   jax.ShapeDtypeStruct((b, N_HEADS, HEAD_DIM, t), F32),
                   jax.ShapeDtypeStruct((b, t, D_SB), BF16),
                   jax.ShapeDtypeStruct((b, t, D_SB), BF16),
                   jax.ShapeDtypeStruct((b, t, 2 * D_MODEL), F32)],
        compiler_params=pltpu.CompilerParams(dimension_semantics=("parallel", "parallel"),
                                             vmem_limit_bytes=VMEM_LIMIT),
        name="inproj",
    )(x, g1, w_all, gq, gk, bd)


def _rwkv_kernel(p_ref, shift_ref, s0_ref, mu_ref, w0_ref, w2_ref, a0_ref, a2_ref, g2_ref,
                 kk_ref, ka_ref, rk_ref, lw_ref, lb_ref, bd_ref, tri_ref,
                 y_ref, sout_ref, carry_sc, s_sc, *, bb, tt):
    c = pl.program_id(1)
    rows = bb * tt

    @pl.when(c == 0)
    def _():
        carry_sc[...] = shift_ref[...]
        s_sc[...] = s0_ref[...]

    p = p_ref[...].reshape(rows, D_RWKV_PAD)
    rolled = pltpu.roll(p, 1, 0)
    first_row = lax.broadcasted_iota(jnp.int32, (SUBLANES, D_RWKV_PAD), 0) == 0
    pieces = []
    for b_ in range(bb):
        lo = b_ * tt
        pieces += [jnp.where(first_row, carry_sc[b_], rolled[lo:lo + SUBLANES]),
                   rolled[lo + SUBLANES:lo + tt]]
        carry_sc[b_] = p[lo + tt - 1:lo + tt, :]
    prev = jnp.concatenate(pieces, axis=0)
    pm = p + (prev - p) * mu_ref[...]
    r = pm[:, 0:D_RWKV]
    k = pm[:, D_RWKV:2 * D_RWKV]
    v = pm[:, 2 * D_RWKV:3 * D_RWKV]
    wl = pm[:, PW_OFF:PW_OFF + LANES]
    al = pm[:, PA_OFF:PA_OFF + LANES]
    gl = pm[:, PG_OFF:PG_OFF + 2 * LANES]

    bd = bd_ref[...]
    w = -_softplus(-(w0_ref[...] + _mm(jnp.tanh(wl), w2_ref[...]))) - 0.5
    ld = -jnp.exp(w)
    a = _sigmoid(a0_ref[...] + _mm(al, a2_ref[...]))
    g = _mm(_sigmoid(gl), g2_ref[...])
    kk = k * kk_ref[...]
    kk = kk * lax.rsqrt(jnp.maximum(_head_sum(kk * kk, bd), L2_EPS))
    k = k * (1.0 + (a - 1.0) * ka_ref[...])
    bonus = _head_sum(r * k * rk_ref[...], bd) * v

    n_ch = tt // CHUNK
    chunks = [(b_, ch) for b_ in range(bb) for ch in range(n_ch)]
    rows_of = lambda b_, ch: slice(b_ * tt + ch * CHUNK, b_ * tt + (ch + 1) * CHUNK)
    cl = _dot_lh(tri_ref[...], ld)
    cl_end = jnp.concatenate(
        [jnp.broadcast_to(cl[rows_of(*c).stop - 1:rows_of(*c).stop, :], (CHUNK, D_RWKV))
         for c in chunks], axis=0)
    w_inv = jnp.exp(-cl)
    w_end = jnp.exp(cl_end - cl)
    at = -kk * jnp.exp(cl - ld)
    bt = kk * a * w_inv
    kt = k * w_inv
    rt = r * jnp.exp(cl)
    be = kk * a * w_end
    ke = k * w_end
    wc = jnp.exp(cl_end)

    lane = lax.broadcasted_iota(jnp.int32, (CHUNK, LANES), 1)
    m0 = lane < HEAD_DIM
    ri = lax.broadcasted_iota(jnp.int32, (LANES, LANES), 0)
    ci = lax.broadcasted_iota(jnp.int32, (LANES, LANES), 1)
    same = (ri // CHUNK) == (ci // CHUNK)
    strict = same & ((ri % CHUNK) > (ci % CHUNK))
    incl = same & ((ri % CHUNK) >= (ci % CHUNK))
    eye = (ri == ci).astype(F32)

    def stack(u):
        zero = jnp.zeros_like(u)
        return jnp.concatenate([jnp.where(m0, u, zero), jnp.where(m0, zero, u)], axis=0)

    stack_b = lambda u: stack(u.astype(BF16))

    keys = [(b_, ch, pr) for (b_, ch) in chunks for pr in range(N_PAIRS)]
    blk = lambda x, key: x[rows_of(key[0], key[1]), key[2] * LANES:(key[2] + 1) * LANES]
    la = {q: stack_b(blk(at, q)) for q in keys}
    lr = {q: stack(blk(rt, q)) for q in keys}
    vst = {q: stack(blk(v, q)) for q in keys}
    twice = lambda u: jnp.concatenate([u, u], axis=0)
    g4 = {q: _mm_nt(jnp.concatenate([la[q], lr[q].astype(BF16)], axis=0),
                    jnp.concatenate([twice(blk(bt, q)), twice(blk(kt, q))], axis=0)) for q in keys}
    a_ab = {q: jnp.where(strict, g4[q][:LANES, :LANES], 0.0) for q in keys}
    a_ak = {q: jnp.where(strict, g4[q][:LANES, LANES:], 0.0) for q in keys}
    a_rb = {q: jnp.where(incl, g4[q][LANES:, :LANES], 0.0) for q in keys}
    a_rk = {q: jnp.where(incl, g4[q][LANES:, LANES:], 0.0) for q in keys}
    tinv = {q: eye + a_ab[q] for q in keys}
    apow = {q: _mm(a_ab[q], a_ab[q]) for q in keys}
    for _ in range(4):
        nxt = {q: _mm(apow[q], jnp.concatenate([apow[q], tinv[q]], axis=1)) for q in keys}
        apow = {q: nxt[q][:, :LANES] for q in keys}
        tinv = {q: tinv[q] + nxt[q][:, LANES:] for q in keys}
    tinv = {q: tinv[q] + _mm(apow[q], tinv[q]) for q in keys}
    akv = {q: _mm(jnp.concatenate([a_ak[q], a_rk[q]], axis=0), vst[q]) for q in keys}
    tlav = {q: _mm(tinv[q], jnp.concatenate([la[q], akv[q][:LANES].astype(BF16)], axis=1))
            for q in keys}
    rlon = {q: _mm(a_rb[q], tlav[q]) for q in keys}
    rl = {q: lr[q] + rlon[q][:, :LANES] for q in keys}
    on = {q: rlon[q][:, LANES:] + akv[q][LANES:] for q in keys}
    mn = {q: _mm(tlav[q].T, stack_b(blk(be, q))) for q in keys}
    nc = {q: mn[q][LANES:] + _mm(vst[q].T, stack_b(blk(ke, q))) for q in keys}

    out_rows = []
    for b_ in range(bb):
        state = [s_sc[b_, pr] for pr in range(N_PAIRS)]
        for ch in range(n_ch):
            outs = []
            for pr in range(N_PAIRS):
                q = (b_, ch, pr)
                s_b = state[pr].astype(BF16)
                o_st = _dot_nt(rl[q].astype(BF16), s_b) + on[q]
                outs.append(o_st[0:CHUNK] + o_st[CHUNK:2 * CHUNK])
                state[pr] = (state[pr] * blk(wc, q)[0:1, :] + _dot(s_b, mn[q][:LANES].astype(BF16))
                             + nc[q])
            out_rows.append(jnp.concatenate(outs, axis=1))
        for pr in range(N_PAIRS):
            s_sc[b_, pr] = state[pr]
            sout_ref[b_, pr] = state[pr]
    o = jnp.concatenate(out_rows, axis=0)

    mean = _head_sum(o, bd) * (1.0 / HEAD_DIM)
    oc = o - mean
    var = _head_sum(oc * oc, bd) * (1.0 / HEAD_DIM)
    o = oc * lax.rsqrt(var + GN_EPS)
    y_ref[...] = ((o * lw_ref[...] + lb_ref[...] + bonus) * g).reshape(bb, tt, D_RWKV)


def _rwkv(p3d, shift_pad, s0_bd, params, bd, tri, bb, tt):
    b, t, _ = p3d.shape
    const = lambda a: pl.BlockSpec(a.shape, lambda i, j: (0,) * a.ndim)
    state = pl.BlockSpec((bb, N_PAIRS, LANES, LANES), lambda i, j: (i, 0, 0, 0))
    return pl.pallas_call(
        functools.partial(_rwkv_kernel, bb=bb, tt=tt),
        grid=(b // bb, t // tt),
        in_specs=[pl.BlockSpec((bb, tt, D_RWKV_PAD), lambda i, j: (i, j, 0)),
                  pl.BlockSpec((bb, 1, D_RWKV_PAD), lambda i, j: (i, 0, 0)), state]
                 + [const(a) for a in params] + [const(bd), const(tri)],
        out_specs=[pl.BlockSpec((bb, tt, D_RWKV), lambda i, j: (i, j, 0)), state],
        out_shape=[jax.ShapeDtypeStruct((b, t, D_RWKV), F32),
                   jax.ShapeDtypeStruct((b, N_PAIRS, LANES, LANES), F32)],
        scratch_shapes=[pltpu.VMEM((bb, 1, D_RWKV_PAD), F32),
                        pltpu.VMEM((bb, N_PAIRS, LANES, LANES), F32)],
        compiler_params=pltpu.CompilerParams(dimension_semantics=("parallel", "arbitrary"),
                                             vmem_limit_bytes=VMEM_LIMIT),
        name="rwkv",
    )(p3d, shift_pad, s0_bd, *params, bd, tri)


def _sb_prompt_kernel(q_ref, k_ref, v_ref, tri_ref, y_ref, qm_sc, acc_sc, carry_sc, *, tq):
    i = pl.program_id(1)
    lane_q = lax.broadcasted_iota(jnp.int32, (tq, LANES), 1) < HEAD_DIM
    causal = (lax.broadcasted_iota(jnp.int32, (tq, tq), 1)
              < lax.broadcasted_iota(jnp.int32, (tq, tq), 0))
    for pr in range(N_PAIRS):
        qp = q_ref[0, :, pr * LANES:(pr + 1) * LANES]
        qm_sc[2 * pr] = jnp.where(lane_q, qp, jnp.zeros_like(qp))
        qm_sc[2 * pr + 1] = jnp.where(lane_q, jnp.zeros_like(qp), qp)

    def block(j, diagonal):
        start = pl.multiple_of(j * tq, tq)
        tri = tri_ref[...]
        heads = range(N_HEADS)
        z = [_dot_nt(qm_sc[h], k_ref[0, pl.ds(start, tq), (h // 2) * LANES:(h // 2 + 1) * LANES])
             for h in heads]
        sp = [_softplus2(z[h]) for h in heads]
        if diagonal:
            sp = [jnp.where(causal, s, 0.0) for s in sp]
            tail = [_dot(sp[h].astype(BF16), tri) for h in heads]
        else:
            tail = [_dot(sp[h].astype(BF16), tri)
                    + jnp.concatenate([carry_sc[h]] * (tq // LANES), axis=1) for h in heads]
        att = [jnp.exp2(z[h] - tail[h]) for h in heads]
        if diagonal:
            att = [jnp.where(causal, a, 0.0) for a in att]
        mass = tail[0][:, 0:1]
        for h in heads:
            carry_sc[h] = jnp.broadcast_to(tail[h][:, 0:1], (tq, LANES))
            mass = jnp.minimum(mass, tail[h][:, 0:1])
        for pr in range(N_PAIRS):
            vblk = v_ref[0, pl.ds(start, tq), pr * LANES:(pr + 1) * LANES]
            v2 = jnp.concatenate([jnp.where(lane_q, vblk, jnp.zeros_like(vblk)),
                                  jnp.where(lane_q, jnp.zeros_like(vblk), vblk)], axis=0)
            att2 = jnp.concatenate([att[2 * pr].astype(BF16), att[2 * pr + 1].astype(BF16)], axis=1)
            pv = _dot(att2, v2)
            acc_sc[pr] = pv if diagonal else acc_sc[pr] + pv
        return jnp.min(mass)

    def body(state):
        jj, _ = state
        return jj + 1, block(i - 1 - jj, False)

    lax.while_loop(lambda s: jnp.logical_and(s[0] < i, s[1] < SKIP_BITS), body,
                   (jnp.int32(0), block(i, True)))
    for pr in range(N_PAIRS):
        y_ref[0, :, pr * LANES:(pr + 1) * LANES] = acc_sc[pr]


def _sb_prompt(qb, kb, vb, tri, tq):
    b, t, _ = qb.shape
    full = pl.BlockSpec((1, t, D_SB), lambda i, j: (i, 0, 0))
    tile = pl.BlockSpec((1, tq, D_SB), lambda i, j: (i, j, 0))
    return pl.pallas_call(
        functools.partial(_sb_prompt_kernel, tq=tq),
        grid=(b, t // tq),
        in_specs=[tile, full, full, pl.BlockSpec(tri.shape, lambda i, j: (0, 0))],
        out_specs=tile,
        out_shape=jax.ShapeDtypeStruct((b, t, D_SB), F32),
        scratch_shapes=[pltpu.VMEM((N_HEADS, tq, LANES), BF16),
                        pltpu.VMEM((N_PAIRS, tq, LANES), F32),
                        pltpu.VMEM((N_HEADS, tq, LANES), F32)],
        compiler_params=pltpu.CompilerParams(dimension_semantics=("parallel", "arbitrary"),
                                             vmem_limit_bytes=VMEM_LIMIT),
        name="sb_prompt",
    )(qb, kb, vb, tri)


def _sb_sample_kernel(q_ref, kn_ref, vn_ref, kct_ref, vct_ref, tri_ref, trin_ref, y_ref):
    t = q_ref.shape[1]
    past = kct_ref.shape[3]
    tri, trin = tri_ref[...], trin_ref[...]
    tk = tri.shape[0]
    nb = past // tk
    causal = (lax.broadcasted_iota(jnp.int32, (t, t), 1) < lax.broadcasted_iota(jnp.int32, (t, t), 0))
    lane_lo = lax.broadcasted_iota(jnp.int32, (t, LANES), 1) < HEAD_DIM
    row_lo = lax.broadcasted_iota(jnp.int32, (LANES, past), 0) < HEAD_DIM
    heads = range(N_HEADS)
    pair = lambda ref, pr: ref[0, :, pr * LANES:(pr + 1) * LANES]
    own = lambda h, u, mask: jnp.where(mask if h % 2 == 0 else jnp.logical_not(mask), u,
                                       jnp.zeros_like(u))
    cache = lambda ref, pr: ref[0, 2 * pr:2 * pr + 2].reshape(LANES, past).astype(BF16)
    kct = [cache(kct_ref, pr) for pr in range(N_PAIRS)]
    vct = [cache(vct_ref, pr) for pr in range(N_PAIRS)]
    qm = [own(h, pair(q_ref, h // 2), lane_lo) for h in heads]
    zn = [_dot_nt(qm[h], pair(kn_ref, h // 2)) for h in heads]
    zp = [_dot(qm[h], kct[h // 2]) for h in heads]
    spn = [jnp.where(causal, _softplus2(z), 0.0) for z in zn]
    spp = [_softplus2(z) for z in zp]
    tailn = [_dot(s.astype(BF16), trin) for s in spn]
    tailp = [[_dot(s[:, c * tk:(c + 1) * tk].astype(BF16), tri) for c in range(nb)] for s in spp]
    for h in heads:
        off = jnp.broadcast_to(tailn[h][:, 0:1], (t, tk))
        for c in reversed(range(nb)):
            tailp[h][c] = tailp[h][c] + off
            off = jnp.broadcast_to(tailp[h][c][:, 0:1], (t, tk))
    out = [None] * N_PAIRS
    for h in heads:
        attn = jnp.where(causal, jnp.exp2(zn[h] - tailn[h]), 0.0).astype(BF16)
        attp = jnp.exp2(zp[h] - jnp.concatenate(tailp[h], axis=1)).astype(BF16)
        yh = (_dot(attn, own(h, pair(vn_ref, h // 2), lane_lo))
              + _dot_nt(attp, own(h, vct[h // 2], row_lo)))
        out[h // 2] = yh if out[h // 2] is None else out[h // 2] + yh
    for pr in range(N_PAIRS):
        y_ref[0, :, pr * LANES:(pr + 1) * LANES] = out[pr]


def _sb_sample(qb, kb, vb, kct, vct, tri, trin):
    b, t, _ = qb.shape
    past = kct.shape[3]
    new = pl.BlockSpec((1, t, D_SB), lambda i: (i, 0, 0))
    old = pl.BlockSpec((1, N_HEADS, HEAD_DIM, past), lambda i: (i, 0, 0, 0))
    const = lambda a: pl.BlockSpec(a.shape, lambda i: (0, 0))
    return pl.pallas_call(
        _sb_sample_kernel,
        grid=(b,),
        in_specs=[new, new, new, old, old, const(tri), const(trin)],
        out_specs=new,
        out_shape=jax.ShapeDtypeStruct((b, t, D_SB), F32),
        compiler_params=pltpu.CompilerParams(dimension_semantics=("parallel",),
                                             vmem_limit_bytes=VMEM_LIMIT),
        name="sb_sample",
    )(qb, kb, vb, kct, vct, tri, trin)


def _merge_ffn_kernel(x_ref, ya_ref, yb_ref, gate_ref, wua_ref, wub_ref, wo_ref, g2_ref,
                      w1_ref, w2_ref, y_ref, *, ff_chunk):
    gate = gate_ref[...]
    merged = (gate[:, 0:D_MODEL] * _dot(ya_ref[...].astype(BF16), wua_ref[...])
              + gate[:, D_MODEL:2 * D_MODEL] * _dot(yb_ref[...].astype(BF16), wub_ref[...]))
    x = x_ref[...] + _dot(merged.astype(BF16), wo_ref[...])
    ms = jnp.mean(x * x, axis=-1, keepdims=True)
    xn = (x * lax.rsqrt(ms + RMS_EPS) * g2_ref[...]).astype(BF16)
    for c in range(D_FF // ff_chunk):
        sl = slice(c * ff_chunk, (c + 1) * ff_chunk)
        h = jnp.maximum(_dot(xn, w1_ref[:, sl]), 0.0)
        x = x + _dot((h * h).astype(BF16), w2_ref[sl, :])
    y_ref[...] = x


def _merge_ffn(x2d, ya, yb, gate, wua, wub, wo, g2, w1, w2, tm, ff_chunk):
    n = x2d.shape[0]
    row = lambda w: pl.BlockSpec((tm, w), lambda i: (i, 0))
    const = lambda a: pl.BlockSpec(a.shape, lambda i: (0, 0), pipeline_mode=pl.Buffered(1))
    return pl.pallas_call(
        functools.partial(_merge_ffn_kernel, ff_chunk=ff_chunk),
        grid=(n // tm,),
        in_specs=[row(D_MODEL), row(D_RWKV), row(D_SB), row(2 * D_MODEL),
                  const(wua), const(wub), const(wo), const(g2), const(w1), const(w2)],
        out_specs=row(D_MODEL),
        out_shape=jax.ShapeDtypeStruct((n, D_MODEL), F32),
        compiler_params=pltpu.CompilerParams(dimension_semantics=("parallel",),
                                             vmem_limit_bytes=VMEM_LIMIT),
        name="merge_ffn",
    )(x2d, ya, yb, gate, wua, wub, wo, g2, w1, w2)


def _pad_cols(a, width):
    return jnp.pad(a, [(0, 0)] * (a.ndim - 1) + [(0, width - a.shape[-1])])


def _pad_rwkv_cols(a):
    o1 = 3 * D_RWKV + D_DECAY_LORA
    o2 = o1 + D_AAA_LORA
    return jnp.concatenate([a[..., :3 * D_RWKV],
                            _pad_cols(a[..., 3 * D_RWKV:o1], LANES),
                            _pad_cols(a[..., o1:o2], LANES),
                            _pad_cols(a[..., o2:], 2 * LANES)], axis=-1)


def _unpad_rwkv_cols(a):
    return jnp.concatenate([a[..., :PW_OFF + D_DECAY_LORA],
                            a[..., PA_OFF:PA_OFF + D_AAA_LORA],
                            a[..., PG_OFF:PG_OFF + D_GATE_LORA]], axis=-1)


def _state_to_pairs(s):
    b = s.shape[0]
    s = s.reshape(b, N_PAIRS, 2, HEAD_DIM, HEAD_DIM)
    z = jnp.zeros_like(s[:, :, 0])
    top = jnp.concatenate([s[:, :, 0], z], axis=-1)
    bot = jnp.concatenate([z, s[:, :, 1]], axis=-1)
    return jnp.concatenate([top, bot], axis=-2)


def _pairs_to_state(sp):
    b = sp.shape[0]
    h0 = sp[:, :, :HEAD_DIM, :HEAD_DIM]
    h1 = sp[:, :, HEAD_DIM:, HEAD_DIM:]
    return jnp.stack([h0, h1], axis=2).reshape(b, N_HEADS, HEAD_DIM, HEAD_DIM)


def _tri_ge(n):
    i = jnp.arange(n)
    return (i[:, None] >= i[None, :]).astype(BF16)


def _layer(x, shift_prev, s0, kt_past, vt_past, wts, consts):
    bsz, t, _ = x.shape
    bb = min(bsz, RWKV_ROWS // CHUNK)
    tt = RWKV_ROWS // bb
    n = bsz * t
    x2d = x.reshape(n, D_MODEL)
    flat = t % DENSE_TILE != 0
    pr3, qb, kt, vt, kb, vb, gate = _inproj(x2d[None] if flat else x, wts["g1"], wts["w_all"],
                                            wts["gq"], wts["gk"], consts["bd"], DENSE_TILE)
    if flat:
        pr3 = pr3.reshape(bsz, t, D_RWKV_PAD)
        qb, kb, vb = (a.reshape(bsz, t, D_SB) for a in (qb, kb, vb))
        unflat = lambda a: a.reshape(N_HEADS, HEAD_DIM, bsz, t).transpose(2, 0, 3, 1)
        k_new, v_new = unflat(kt), unflat(vt)
    else:
        k_new, v_new = jnp.swapaxes(kt, 2, 3), jnp.swapaxes(vt, 2, 3)
    gate = gate.reshape(n, 2 * D_MODEL)
    shift_new = _unpad_rwkv_cols(pr3[:, t - 1:t, :])
    ya, s_pairs = _rwkv(pr3, _pad_rwkv_cols(shift_prev), _state_to_pairs(s0.astype(F32)),
                        wts["rwkv"], consts["bd"], consts["tri_chunks"], bb, tt)
    if kt_past is None:
        yb = _sb_prompt(qb, kb, vb, consts["tri"], TILE)
    else:
        yb = _sb_sample(qb, kb, vb, kt_past, vt_past, consts["tri"], consts["tri_new"])
    y = _merge_ffn(x2d, ya.reshape(n, D_RWKV), yb.reshape(n, D_SB), gate, wts["wua"], wts["wub"],
                   wts["wo"], wts["g2"], wts["w1"], wts["w2"], DENSE_TILE, FF_CHUNK)
    return y.reshape(bsz, t, D_MODEL), shift_new, _pairs_to_state(s_pairs), k_new, v_new


def _prep_weights(g_norm1, w_in, rwkv_mu, rwkv_w0, rwkv_w2, rwkv_a0, rwkv_a2, rwkv_g2, rwkv_k_k,
                  rwkv_k_a, rwkv_r_k, rwkv_lnx_w, rwkv_lnx_b, sb_q_norm_g, sb_k_norm_g, w_up_a,
                  w_up_b, w_out, g_norm2, w_ff1, w_ff2):
    row = lambda a: a.reshape(1, -1).astype(F32)
    w_all = jnp.concatenate([_pad_rwkv_cols(w_in[:, :D_RWKV_IN]), w_in[:, D_RWKV_IN:]], axis=1)
    pad_rows = lambda a, rows: jnp.pad(a, ((0, rows - a.shape[0]), (0, 0))).astype(BF16)
    rwkv = (row(_pad_rwkv_cols(rwkv_mu)), row(rwkv_w0), pad_rows(rwkv_w2, LANES), row(rwkv_a0),
            pad_rows(rwkv_a2, LANES), pad_rows(rwkv_g2, 2 * LANES), row(rwkv_k_k), row(rwkv_k_a),
            row(rwkv_r_k), row(rwkv_lnx_w), row(rwkv_lnx_b))
    return {
        "g1": row(g_norm1), "w_all": w_all.astype(BF16),
        "gq": row(jnp.tile(sb_q_norm_g, N_HEADS)), "gk": row(jnp.tile(sb_k_norm_g, N_HEADS)),
        "rwkv": rwkv,
        "wua": w_up_a.astype(BF16), "wub": w_up_b.astype(BF16), "wo": w_out.astype(BF16),
        "g2": row(g_norm2), "w1": w_ff1.astype(BF16), "w2": w_ff2.astype(BF16),
    }


def _constants(t_new):
    hd = jnp.arange(2 * LANES) // HEAD_DIM
    ch = jnp.arange(RWKV_ROWS) // CHUNK
    return {
        "bd": (hd[:, None] == hd[None, :]).astype(BF16),
        "tri_chunks": _tri_ge(RWKV_ROWS) * (ch[:, None] == ch[None, :]).astype(BF16),
        "tri": _tri_ge(TILE),
        "tri_new": _tri_ge(t_new),
    }


def kernel(x_prompt, x_sample, cache_sb_k, cache_sb_v, state_rwkv_wkv, state_rwkv_shift, g_norm1, w_in, rwkv_mu, rwkv_w0, rwkv_w2, rwkv_a0, rwkv_a2, rwkv_g2, rwkv_k_k, rwkv_k_a, rwkv_r_k, rwkv_lnx_w, rwkv_lnx_b, sb_q_norm_g, sb_k_norm_g, w_up_a, w_up_b, w_out, g_norm2, w_ff1, w_ff2):
    depth = w_in.shape[0]
    assert depth == 1
    layer_w = (g_norm1, w_in, rwkv_mu, rwkv_w0, rwkv_w2, rwkv_a0, rwkv_a2, rwkv_g2, rwkv_k_k,
               rwkv_k_a, rwkv_r_k, rwkv_lnx_w, rwkv_lnx_b, sb_q_norm_g, sb_k_norm_g, w_up_a,
               w_up_b, w_out, g_norm2, w_ff1, w_ff2)
    wts = _prep_weights(*(a[0] for a in layer_w))
    consts = _constants(x_sample.shape[1])
    bp = x_prompt.shape[0]
    s0_p = jnp.zeros((bp, N_HEADS, HEAD_DIM, HEAD_DIM), F32)
    shift0_p = jnp.zeros((bp, 1, D_RWKV_IN), x_prompt.dtype)
    y_p, sh_p, s_p, k_p, v_p = _layer(x_prompt, shift0_p, s0_p, None, None, wts, consts)
    y_s, sh_s, s_s, k_s, v_s = _layer(x_sample, state_rwkv_shift[0], state_rwkv_wkv[0],
                                      jnp.swapaxes(cache_sb_k[0], 2, 3),
                                      jnp.swapaxes(cache_sb_v[0], 2, 3), wts, consts)
    return (y_p, y_s, k_p[None], v_p[None], s_p[None], sh_p[None],
            k_s[None], v_s[None], s_s[None], sh_s[None])
```

```python
import functools

import jax
import jax.numpy as jnp
from jax import lax
from jax.experimental import pallas as pl
from jax.experimental.pallas import tpu as pltpu

F32 = jnp.float32
BF16 = jnp.bfloat16

D_MODEL = 1024
HEAD_DIM = 64
D_RWKV = 512
D_SB = 512
N_HEADS = 8
N_PAIRS = 4
LANES = 128
SUBLANES = 8
D_DECAY_LORA = 64
D_AAA_LORA = 64
D_GATE_LORA = 160
D_RWKV_IN = 3 * D_RWKV + D_DECAY_LORA + D_AAA_LORA + D_GATE_LORA
D_FF = 4 * D_MODEL
CHUNK = 64
RWKV_ROWS = 256
TILE = 256
SAMPLE_SEQS = 2
DENSE_TILE = 512
FF_CHUNK = 1024
SB_SCALE = HEAD_DIM ** -0.5
LOG2E = 1.4426950408889634
Q_SCALE = SB_SCALE * LOG2E
SKIP_BITS = 150.0
RMS_EPS = 1e-6
GN_EPS = 64e-5
L2_EPS = 1e-24

PW_OFF = 3 * D_RWKV
PA_OFF = PW_OFF + LANES
PG_OFF = PA_OFF + LANES
D_RWKV_PAD = PG_OFF + 2 * LANES
SB_OFF = D_RWKV_PAD
GATE_OFF = SB_OFF + 3 * D_SB
D_IN_PAD = GATE_OFF + 2 * D_MODEL

VMEM_LIMIT = 56 * 1024 * 1024


def _dot(a, b):
    return jnp.dot(a, b, preferred_element_type=F32)


def _dot_nt(a, b):
    return lax.dot_general(a, b, (((1,), (1,)), ((), ())), preferred_element_type=F32)


def _split(x):
    hi = x.astype(BF16)
    lo = (x - hi.astype(F32)).astype(BF16)
    return hi, lo


def _dot_lh(m, x):
    hi, lo = _split(x)
    return _dot(m, hi) + _dot(m, lo)


def _head_sum(x, bd):
    xb = x.astype(BF16)
    w = bd.shape[0]
    return jnp.concatenate([_dot(xb[:, c * w:(c + 1) * w], bd) for c in range(x.shape[1] // w)],
                           axis=1)


def _mm(a, b):
    return _dot(a.astype(BF16), b.astype(BF16))


def _mm_nt(a, b):
    return _dot_nt(a.astype(BF16), b.astype(BF16))


def _softplus(z):
    return jnp.maximum(z, 0.0) + jnp.log(1.0 + jnp.exp(-jnp.abs(z)))


def _softplus2(z):
    return jnp.maximum(z, 0.0) + jnp.log2(1.0 + jnp.exp2(-jnp.abs(z)))


def _sigmoid(z):
    return 0.5 * jnp.tanh(0.5 * z) + 0.5


def _inproj_kernel(x_ref, g1_ref, w_ref, gq_ref, gk_ref, bd_ref,
                   pr_ref, q_ref, kh_ref, vh_ref, kb_ref, vb_ref, gate_ref):
    x = x_ref[0]
    ms = jnp.mean(x * x, axis=-1, keepdims=True)
    xn = (x * lax.rsqrt(ms + RMS_EPS) * g1_ref[...]).astype(BF16)
    pr_ref[0] = _dot(xn, w_ref[:, 0:D_RWKV_PAD])
    bd = bd_ref[...]

    def head_norm(u, g):
        ss = _head_sum(u * u, bd) * (1.0 / HEAD_DIM)
        return u * lax.rsqrt(ss + RMS_EPS) * g

    q = head_norm(_dot(xn, w_ref[:, SB_OFF:SB_OFF + D_SB]), gq_ref[...])
    k = head_norm(_dot(xn, w_ref[:, SB_OFF + D_SB:SB_OFF + 2 * D_SB]), gk_ref[...])
    v = _dot(xn, w_ref[:, SB_OFF + 2 * D_SB:SB_OFF + 3 * D_SB])
    q_ref[0] = (q * Q_SCALE).astype(BF16)
    kb_ref[0] = k.astype(BF16)
    vb_ref[0] = v.astype(BF16)
    kh_ref[0] = k.T.reshape(N_HEADS, HEAD_DIM, k.shape[0])
    vh_ref[0] = v.T.reshape(N_HEADS, HEAD_DIM, v.shape[0])
    gate_ref[0] = _sigmoid(_dot(xn, w_ref[:, GATE_OFF:GATE_OFF + 2 * D_MODEL]))


def _inproj(x, g1, w_all, gq, gk, bd, tm):
    b, t, _ = x.shape
    row = lambda w: pl.BlockSpec((1, tm, w), lambda i, j: (i, j, 0))
    heads = pl.BlockSpec((1, N_HEADS, HEAD_DIM, tm), lambda i, j: (i, 0, 0, j))
    const = lambda a: pl.BlockSpec(a.shape, lambda i, j: (0, 0), pipeline_mode=pl.Buffered(1))
    return pl.pallas_call(
        _inproj_kernel,
        grid=(b, t // tm),
        in_specs=[row(D_MODEL), const(g1), const(w_all), const(gq), const(gk), const(bd)],
        out_specs=[row(D_RWKV_PAD), row(D_SB), heads, heads, row(D_SB), row(D_SB),
                   row(2 * D_MODEL)],
        out_shape=[jax.ShapeDtypeStruct((b, t, D_RWKV_PAD), F32),
                   jax.ShapeDtypeStruct((b, t, D_SB), BF16),
                   jax.ShapeDtypeStruct((b, N_HEADS, HEAD_DIM, t), F32),
                   jax.ShapeDtypeStruct((b, N_HEADS, HEAD_DIM, t), F32),
                   jax.ShapeDtypeStruct((b, t, D_SB), BF16),
                   jax.ShapeDtypeStruct((b, t, D_SB), BF16),
                   jax.ShapeDtypeStruct((b, t, 2 * D_MODEL), F32)],
        compiler_params=pltpu.CompilerParams(dimension_semantics=("parallel", "parallel"),
                                             vmem_limit_bytes=VMEM_LIMIT),
        name="inproj",
    )(x, g1, w_all, gq, gk, bd)


def _rwkv_kernel(p_ref, shift_ref, s0_ref, mu_ref, w0_ref, w2_ref, a0_ref, a2_ref, g2_ref,
                 kk_ref, ka_ref, rk_ref, lw_ref, lb_ref, bd_ref, tri_ref,
                 y_ref, sout_ref, carry_sc, s_sc, *, bb, tt):
    c = pl.program_id(1)
    rows = bb * tt

    @pl.when(c == 0)
    def _():
        carry_sc[...] = shift_ref[...]
        s_sc[...] = s0_ref[...]

    p = p_ref[...].reshape(rows, D_RWKV_PAD)
    rolled = pltpu.roll(p, 1, 0)
    first_row = lax.broadcasted_iota(jnp.int32, (SUBLANES, D_RWKV_PAD), 0) == 0
    pieces = []
    for b_ in range(bb):
        lo = b_ * tt
        pieces += [jnp.where(first_row, carry_sc[b_], rolled[lo:lo + SUBLANES]),
                   rolled[lo + SUBLANES:lo + tt]]
        carry_sc[b_] = p[lo + tt - 1:lo + tt, :]
    prev = jnp.concatenate(pieces, axis=0)
    pm = p + (prev - p) * mu_ref[...]
    r = pm[:, 0:D_RWKV]
    k = pm[:, D_RWKV:2 * D_RWKV]
    v = pm[:, 2 * D_RWKV:3 * D_RWKV]
    wl = pm[:, PW_OFF:PW_OFF + LANES]
    al = pm[:, PA_OFF:PA_OFF + LANES]
    gl = pm[:, PG_OFF:PG_OFF + 2 * LANES]

    bd = bd_ref[...]
    w = -_softplus(-(w0_ref[...] + _mm(jnp.tanh(wl), w2_ref[...]))) - 0.5
    ld = -jnp.exp(w)
    a = _sigmoid(a0_ref[...] + _mm(al, a2_ref[...]))
    g = _mm(_sigmoid(gl), g2_ref[...])
    kk = k * kk_ref[...]
    kk = kk * lax.rsqrt(jnp.maximum(_head_sum(kk * kk, bd), L2_EPS))
    k = k * (1.0 + (a - 1.0) * ka_ref[...])
    bonus = _head_sum(r * k * rk_ref[...], bd) * v

    n_ch = tt // CHUNK
    chunks = [(b_, ch) for b_ in range(bb) for ch in range(n_ch)]
    rows_of = lambda b_, ch: slice(b_ * tt + ch * CHUNK, b_ * tt + (ch + 1) * CHUNK)
    cl = _dot_lh(tri_ref[...], ld)
    cl_end = jnp.concatenate(
        [jnp.broadcast_to(cl[rows_of(*c).stop - 1:rows_of(*c).stop, :], (CHUNK, D_RWKV))
         for c in chunks], axis=0)
    w_inv = jnp.exp(-cl)
    w_end = jnp.exp(cl_end - cl)
    at = -kk * jnp.exp(cl - ld)
    bt = kk * a * w_inv
    kt = k * w_inv
    rt = r * jnp.exp(cl)
    be = kk * a * w_end
    ke = k * w_end
    wc = jnp.exp(cl_end)

    lane = lax.broadcasted_iota(jnp.int32, (CHUNK, LANES), 1)
    m0 = lane < HEAD_DIM
    ri = lax.broadcasted_iota(jnp.int32, (LANES, LANES), 0)
    ci = lax.broadcasted_iota(jnp.int32, (LANES, LANES), 1)
    same = (ri // CHUNK) == (ci // CHUNK)
    strict = same & ((ri % CHUNK) > (ci % CHUNK))
    incl = same & ((ri % CHUNK) >= (ci % CHUNK))
    eye = (ri == ci).astype(F32)

    def stack(u):
        zero = jnp.zeros_like(u)
        return jnp.concatenate([jnp.where(m0, u, zero), jnp.where(m0, zero, u)], axis=0)

    stack_b = lambda u: stack(u.astype(BF16))

    keys = [(b_, ch, pr) for (b_, ch) in chunks for pr in range(N_PAIRS)]
    blk = lambda x, key: x[rows_of(key[0], key[1]), key[2] * LANES:(key[2] + 1) * LANES]
    la = {q: stack_b(blk(at, q)) for q in keys}
    lr = {q: stack(blk(rt, q)) for q in keys}
    vst = {q: stack(blk(v, q)) for q in keys}
    twice = lambda u: jnp.concatenate([u, u], axis=0)
    g4 = {q: _mm_nt(jnp.concatenate([la[q], lr[q].astype(BF16)], axis=0),
                    jnp.concatenate([twice(blk(bt, q)), twice(blk(kt, q))], axis=0)) for q in keys}
    a_ab = {q: jnp.where(strict, g4[q][:LANES, :LANES], 0.0) for q in keys}
    a_ak = {q: jnp.where(strict, g4[q][:LANES, LANES:], 0.0) for q in keys}
    a_rb = {q: jnp.where(incl, g4[q][LANES:, :LANES], 0.0) for q in keys}
    a_rk = {q: jnp.where(incl, g4[q][LANES:, LANES:], 0.0) for q in keys}
    tinv = {q: eye + a_ab[q] for q in keys}
    apow = {q: _mm(a_ab[q], a_ab[q]) for q in keys}
    for _ in range(4):
        nxt = {q: _mm(apow[q], jnp.concatenate([apow[q], tinv[q]], axis=1)) for q in keys}
        apow = {q: nxt[q][:, :LANES] for q in keys}
        tinv = {q: tinv[q] + nxt[q][:, LANES:] for q in keys}
    tinv = {q: tinv[q] + _mm(apow[q], tinv[q]) for q in keys}
    akv = {q: _mm(jnp.concatenate([a_ak[q], a_rk[q]], axis=0), vst[q]) for q in keys}
    tlav = {q: _mm(tinv[q], jnp.concatenate([la[q], akv[q][:LANES].astype(BF16)], axis=1))
            for q in keys}
    rlon = {q: _mm(a_rb[q], tlav[q]) for q in keys}
    rl = {q: lr[q] + rlon[q][:, :LANES] for q in keys}
    on = {q: rlon[q][:, LANES:] + akv[q][LANES:] for q in keys}
    mn = {q: _mm(tlav[q].T, stack_b(blk(be, q))) for q in keys}
    nc = {q: mn[q][LANES:] + _mm(vst[q].T, stack_b(blk(ke, q))) for q in keys}

    out_rows = []
    for b_ in range(bb):
        state = [s_sc[b_, pr] for pr in range(N_PAIRS)]
        for ch in range(n_ch):
            outs = []
            for pr in range(N_PAIRS):
                q = (b_, ch, pr)
                s_b = state[pr].astype(BF16)
                o_st = _dot_nt(rl[q].astype(BF16), s_b) + on[q]
                outs.append(o_st[0:CHUNK] + o_st[CHUNK:2 * CHUNK])
                state[pr] = (state[pr] * blk(wc, q)[0:1, :] + _dot(s_b, mn[q][:LANES].astype(BF16))
                             + nc[q])
            out_rows.append(jnp.concatenate(outs, axis=1))
        for pr in range(N_PAIRS):
            s_sc[b_, pr] = state[pr]
            sout_ref[b_, pr] = state[pr]
    o = jnp.concatenate(out_rows, axis=0)

    mean = _head_sum(o, bd) * (1.0 / HEAD_DIM)
    oc = o - mean
    var = _head_sum(oc * oc, bd) * (1.0 / HEAD_DIM)
    o = oc * lax.rsqrt(var + GN_EPS)
    y_ref[...] = ((o * lw_ref[...] + lb_ref[...] + bonus) * g).reshape(bb, tt, D_RWKV)


def _rwkv(p3d, shift_pad, s0_bd, params, bd, tri, bb, tt):
    b, t, _ = p3d.shape
    const = lambda a: pl.BlockSpec(a.shape, lambda i, j: (0,) * a.ndim)
    state = pl.BlockSpec((bb, N_PAIRS, LANES, LANES), lambda i, j: (i, 0, 0, 0))
    return pl.pallas_call(
        functools.partial(_rwkv_kernel, bb=bb, tt=tt),
        grid=(b // bb, t // tt),
        in_specs=[pl.BlockSpec((bb, tt, D_RWKV_PAD), lambda i, j: (i, j, 0)),
                  pl.BlockSpec((bb, 1, D_RWKV_PAD), lambda i, j: (i, 0, 0)), state]
                 + [const(a) for a in params] + [const(bd), const(tri)],
        out_specs=[pl.BlockSpec((bb, tt, D_RWKV), lambda i, j: (i, j, 0)), state],
        out_shape=[jax.ShapeDtypeStruct((b, t, D_RWKV), F32),
                   jax.ShapeDtypeStruct((b, N_PAIRS, LANES, LANES), F32)],
        scratch_shapes=[pltpu.VMEM((bb, 1, D_RWKV_PAD), F32),
                        pltpu.VMEM((bb, N_PAIRS, LANES, LANES), F32)],
        compiler_params=pltpu.CompilerParams(dimension_semantics=("parallel", "arbitrary"),
                                             vmem_limit_bytes=VMEM_LIMIT),
        name="rwkv",
    )(p3d, shift_pad, s0_bd, *params, bd, tri)


def _sb_prompt_kernel(q_ref, k_ref, v_ref, tri_ref, y_ref, qm_sc, acc_sc, carry_sc, *, tq):
    i = pl.program_id(1)
    half = tq // 2
    lane_q = lax.broadcasted_iota(jnp.int32, (tq, LANES), 1) < HEAD_DIM
    causal = (lax.broadcasted_iota(jnp.int32, (half, half), 1)
              < lax.broadcasted_iota(jnp.int32, (half, half), 0))
    for pr in range(N_PAIRS):
        qp = q_ref[0, :, pr * LANES:(pr + 1) * LANES]
        qm_sc[2 * pr] = jnp.where(lane_q, qp, jnp.zeros_like(qp))
        qm_sc[2 * pr + 1] = jnp.where(lane_q, jnp.zeros_like(qp), qp)

    heads = range(N_HEADS)
    k_pair = lambda h, start, n: k_ref[0, pl.ds(start, n), (h // 2) * LANES:(h // 2 + 1) * LANES]

    def set_mass(cols):
        mass = cols[0]
        for h in heads:
            carry_sc[h] = jnp.broadcast_to(cols[h], (tq, LANES))
            mass = jnp.minimum(mass, cols[h])
        return jnp.min(mass)

    def pv_pair(pr, att, start, n_keys):
        vblk = v_ref[0, pl.ds(start, n_keys), pr * LANES:(pr + 1) * LANES]
        lo = lax.broadcasted_iota(jnp.int32, (n_keys, LANES), 1) < HEAD_DIM
        v2 = jnp.concatenate([jnp.where(lo, vblk, jnp.zeros_like(vblk)),
                              jnp.where(lo, jnp.zeros_like(vblk), vblk)], axis=0)
        att2 = jnp.concatenate([att[2 * pr].astype(BF16), att[2 * pr + 1].astype(BF16)], axis=1)
        return _dot(att2, v2)

    def block(j):
        start = pl.multiple_of(j * tq, tq)
        tri = tri_ref[...]
        z = [_dot_nt(qm_sc[h], k_pair(h, start, tq)) for h in heads]
        sp = [_softplus2(z[h]) for h in heads]
        tail = [_dot(sp[h].astype(BF16), tri)
                + jnp.concatenate([carry_sc[h]] * (tq // LANES), axis=1) for h in heads]
        att = [jnp.exp2(z[h] - tail[h]) for h in heads]
        for pr in range(N_PAIRS):
            acc_sc[pr] += pv_pair(pr, att, start, tq)
        return set_mass([t_[:, 0:1] for t_ in tail])

    def first_blocks(with_prev):
        start = pl.multiple_of(i * tq, tq)
        prev = pl.multiple_of(jnp.maximum(i - 1, 0) * tq, tq)
        tri = tri_ref[...]
        keep = lambda u: jnp.where(causal, u, 0.0)
        right = lambda u: jnp.concatenate([u[:, :half], keep(u[:, half:])], axis=1)
        z_t = [_dot_nt(qm_sc[h, :half], k_pair(h, start, half)) for h in heads]
        z_b = [_dot_nt(qm_sc[h, half:], k_pair(h, start, tq)) for h in heads]
        z_p = [_dot_nt(qm_sc[h], k_pair(h, prev, tq)) for h in heads] if with_prev else []
        sp_t = [keep(_softplus2(z)) for z in z_t]
        sp_b = [right(_softplus2(z)) for z in z_b]
        sp_p = [_softplus2(z) for z in z_p]
        tail_t = [_dot(s.astype(BF16), tri[:half, :half]) for s in sp_t]
        tail_b = [_dot(s.astype(BF16), tri) for s in sp_b]
        tail_p = [_dot(s.astype(BF16), tri) for s in sp_p]
        att_t = [keep(jnp.exp2(z_t[h] - tail_t[h])) for h in heads]
        att_b = [right(jnp.exp2(z_b[h] - tail_b[h])) for h in heads]
        cols = [jnp.concatenate([tail_t[h][:, 0:1], tail_b[h][:, 0:1]], axis=0) for h in heads]
        if with_prev:
            tail_p = [tail_p[h] + cols[h] for h in heads]
            att_p = [jnp.exp2(z_p[h] - tail_p[h]) for h in heads]
            cols = [t_[:, 0:1] for t_ in tail_p]
        for pr in range(N_PAIRS):
            pv = jnp.concatenate([pv_pair(pr, att_t, start, half), pv_pair(pr, att_b, start, tq)], axis=0)
            acc_sc[pr] = pv + pv_pair(pr, att_p, prev, tq) if with_prev else pv
        return set_mass(cols)

    @pl.when(i == 0)
    def _():
        first_blocks(False)

    @pl.when(i > 0)
    def _():
        def body(state):
            jj, _ = state
            return jj + 1, block(i - 2 - jj)

        lax.while_loop(lambda s: jnp.logical_and(s[0] < i - 1, s[1] < SKIP_BITS), body,
                       (jnp.int32(0), first_blocks(True)))

    for pr in range(N_PAIRS):
        y_ref[0, :, pr * LANES:(pr + 1) * LANES] = acc_sc[pr]


def _sb_prompt(qb, kb, vb, tri, tq):
    b, t, _ = qb.shape
    full = pl.BlockSpec((1, t, D_SB), lambda i, j: (i, 0, 0))
    tile = pl.BlockSpec((1, tq, D_SB), lambda i, j: (i, j, 0))
    return pl.pallas_call(
        functools.partial(_sb_prompt_kernel, tq=tq),
        grid=(b, t // tq),
        in_specs=[tile, full, full, pl.BlockSpec(tri.shape, lambda i, j: (0, 0))],
        out_specs=tile,
        out_shape=jax.ShapeDtypeStruct((b, t, D_SB), F32),
        scratch_shapes=[pltpu.VMEM((N_HEADS, tq, LANES), BF16),
                        pltpu.VMEM((N_PAIRS, tq, LANES), F32),
                        pltpu.VMEM((N_HEADS, tq, LANES), F32)],
        compiler_params=pltpu.CompilerParams(dimension_semantics=("parallel", "arbitrary"),
                                             vmem_limit_bytes=VMEM_LIMIT),
        name="sb_prompt",
    )(qb, kb, vb, tri)


def _sb_sample_kernel(q_ref, kn_ref, vn_ref, kct_ref, vct_ref, tri_ref, trin_ref, y_ref):
    t = q_ref.shape[1]
    past = kct_ref.shape[3]
    tri, trin = tri_ref[...], trin_ref[...]
    tk = tri.shape[0]
    nb = past // tk
    causal = (lax.broadcasted_iota(jnp.int32, (t, t), 1) < lax.broadcasted_iota(jnp.int32, (t, t), 0))
    lane_lo = lax.broadcasted_iota(jnp.int32, (t, LANES), 1) < HEAD_DIM
    row_lo = lax.broadcasted_iota(jnp.int32, (LANES, past), 0) < HEAD_DIM
    units = [(s, h) for s in range(q_ref.shape[0]) for h in range(N_HEADS)]
    pair = lambda ref, u: ref[u[0], :, (u[1] // 2) * LANES:(u[1] // 2 + 1) * LANES]
    own = lambda u, x, mask: jnp.where(mask if u[1] % 2 == 0 else jnp.logical_not(mask), x,
                                       jnp.zeros_like(x))
    cache = lambda ref, u: ref[u[0], 2 * (u[1] // 2):2 * (u[1] // 2) + 2].reshape(LANES, past).astype(BF16)
    qm = {u: own(u, pair(q_ref, u), lane_lo) for u in units}
    zn = {u: _dot_nt(qm[u], pair(kn_ref, u)) for u in units}
    zp = {u: _dot(qm[u], cache(kct_ref, u)) for u in units}
    spn = {u: jnp.where(causal, _softplus2(zn[u]), 0.0) for u in units}
    spp = {u: _softplus2(zp[u]) for u in units}
    tailn = {u: _dot(spn[u].astype(BF16), trin) for u in units}
    tailp = {u: [_dot(spp[u][:, c * tk:(c + 1) * tk].astype(BF16), tri) for c in range(nb)]
             for u in units}
    for u in units:
        off = jnp.broadcast_to(tailn[u][:, 0:1], (t, tk))
        for c in reversed(range(nb)):
            tailp[u][c] = tailp[u][c] + off
            off = jnp.broadcast_to(tailp[u][c][:, 0:1], (t, tk))
    out = {}
    for u in units:
        attn = jnp.where(causal, jnp.exp2(zn[u] - tailn[u]), 0.0).astype(BF16)
        attp = jnp.exp2(zp[u] - jnp.concatenate(tailp[u], axis=1)).astype(BF16)
        yh = (_dot(attn, own(u, pair(vn_ref, u), lane_lo))
              + _dot_nt(attp, own(u, cache(vct_ref, u), row_lo)))
        key = (u[0], u[1] // 2)
        out[key] = yh if key not in out else out[key] + yh
    for (s, pr), val in out.items():
        y_ref[s, :, pr * LANES:(pr + 1) * LANES] = val


def _sb_sample(qb, kb, vb, kct, vct, tri, trin):
    b, t, _ = qb.shape
    past = kct.shape[3]
    ns = SAMPLE_SEQS
    new = pl.BlockSpec((ns, t, D_SB), lambda i: (i, 0, 0))
    old = pl.BlockSpec((ns, N_HEADS, HEAD_DIM, past), lambda i: (i, 0, 0, 0))
    const = lambda a: pl.BlockSpec(a.shape, lambda i: (0, 0))
    return pl.pallas_call(
        _sb_sample_kernel,
        grid=(b // ns,),
        in_specs=[new, new, new, old, old, const(tri), const(trin)],
        out_specs=new,
        out_shape=jax.ShapeDtypeStruct((b, t, D_SB), F32),
        compiler_params=pltpu.CompilerParams(dimension_semantics=("parallel",),
                                             vmem_limit_bytes=VMEM_LIMIT),
        name="sb_sample",
    )(qb, kb, vb, kct, vct, tri, trin)


def _merge_ffn_kernel(x_ref, ya_ref, yb_ref, gate_ref, wua_ref, wub_ref, wo_ref, g2_ref,
                      w1_ref, w2_ref, y_ref, *, ff_chunk):
    gate = gate_ref[...]
    merged = (gate[:, 0:D_MODEL] * _dot(ya_ref[...].astype(BF16), wua_ref[...])
              + gate[:, D_MODEL:2 * D_MODEL] * _dot(yb_ref[...].astype(BF16), wub_ref[...]))
    x = x_ref[...] + _dot(merged.astype(BF16), wo_ref[...])
    ms = jnp.mean(x * x, axis=-1, keepdims=True)
    xn = (x * lax.rsqrt(ms + RMS_EPS) * g2_ref[...]).astype(BF16)
    for c in range(D_FF // ff_chunk):
        sl = slice(c * ff_chunk, (c + 1) * ff_chunk)
        h = jnp.maximum(_dot(xn, w1_ref[:, sl]), 0.0)
        x = x + _dot((h * h).astype(BF16), w2_ref[sl, :])
    y_ref[...] = x


def _merge_ffn(x2d, ya, yb, gate, wua, wub, wo, g2, w1, w2, tm, ff_chunk):
    n = x2d.shape[0]
    row = lambda w: pl.BlockSpec((tm, w), lambda i: (i, 0))
    const = lambda a: pl.BlockSpec(a.shape, lambda i: (0, 0), pipeline_mode=pl.Buffered(1))
    return pl.pallas_call(
        functools.partial(_merge_ffn_kernel, ff_chunk=ff_chunk),
        grid=(n // tm,),
        in_specs=[row(D_MODEL), row(D_RWKV), row(D_SB), row(2 * D_MODEL),
                  const(wua), const(wub), const(wo), const(g2), const(w1), const(w2)],
        out_specs=row(D_MODEL),
        out_shape=jax.ShapeDtypeStruct((n, D_MODEL), F32),
        compiler_params=pltpu.CompilerParams(dimension_semantics=("parallel",),
                                             vmem_limit_bytes=VMEM_LIMIT),
        name="merge_ffn",
    )(x2d, ya, yb, gate, wua, wub, wo, g2, w1, w2)


def _pad_cols(a, width):
    return jnp.pad(a, [(0, 0)] * (a.ndim - 1) + [(0, width - a.shape[-1])])


def _pad_rwkv_cols(a):
    o1 = 3 * D_RWKV + D_DECAY_LORA
    o2 = o1 + D_AAA_LORA
    return jnp.concatenate([a[..., :3 * D_RWKV],
                            _pad_cols(a[..., 3 * D_RWKV:o1], LANES),
                            _pad_cols(a[..., o1:o2], LANES),
                            _pad_cols(a[..., o2:], 2 * LANES)], axis=-1)


def _unpad_rwkv_cols(a):
    return jnp.concatenate([a[..., :PW_OFF + D_DECAY_LORA],
                            a[..., PA_OFF:PA_OFF + D_AAA_LORA],
                            a[..., PG_OFF:PG_OFF + D_GATE_LORA]], axis=-1)


def _state_to_pairs(s):
    b = s.shape[0]
    s = s.reshape(b, N_PAIRS, 2, HEAD_DIM, HEAD_DIM)
    z = jnp.zeros_like(s[:, :, 0])
    top = jnp.concatenate([s[:, :, 0], z], axis=-1)
    bot = jnp.concatenate([z, s[:, :, 1]], axis=-1)
    return jnp.concatenate([top, bot], axis=-2)


def _pairs_to_state(sp):
    b = sp.shape[0]
    h0 = sp[:, :, :HEAD_DIM, :HEAD_DIM]
    h1 = sp[:, :, HEAD_DIM:, HEAD_DIM:]
    return jnp.stack([h0, h1], axis=2).reshape(b, N_HEADS, HEAD_DIM, HEAD_DIM)


def _tri_ge(n):
    i = jnp.arange(n)
    return (i[:, None] >= i[None, :]).astype(BF16)


def _layer(x, shift_prev, s0, kt_past, vt_past, wts, consts):
    bsz, t, _ = x.shape
    bb = min(bsz, RWKV_ROWS // CHUNK)
    tt = RWKV_ROWS // bb
    n = bsz * t
    x2d = x.reshape(n, D_MODEL)
    flat = t % DENSE_TILE != 0
    pr3, qb, kt, vt, kb, vb, gate = _inproj(x2d[None] if flat else x, wts["g1"], wts["w_all"],
                                            wts["gq"], wts["gk"], consts["bd"], DENSE_TILE)
    if flat:
        pr3 = pr3.reshape(bsz, t, D_RWKV_PAD)
        qb, kb, vb = (a.reshape(bsz, t, D_SB) for a in (qb, kb, vb))
        unflat = lambda a: a.reshape(N_HEADS, HEAD_DIM, bsz, t).transpose(2, 0, 3, 1)
        k_new, v_new = unflat(kt), unflat(vt)
    else:
        k_new, v_new = jnp.swapaxes(kt, 2, 3), jnp.swapaxes(vt, 2, 3)
    gate = gate.reshape(n, 2 * D_MODEL)
    shift_new = _unpad_rwkv_cols(pr3[:, t - 1:t, :])
    ya, s_pairs = _rwkv(pr3, _pad_rwkv_cols(shift_prev), _state_to_pairs(s0.astype(F32)),
                        wts["rwkv"], consts["bd"], consts["tri_chunks"], bb, tt)
    if kt_past is None:
        yb = _sb_prompt(qb, kb, vb, consts["tri"], TILE)
    else:
        yb = _sb_sample(qb, kb, vb, kt_past, vt_past, consts["tri"], consts["tri_new"])
    y = _merge_ffn(x2d, ya.reshape(n, D_RWKV), yb.reshape(n, D_SB), gate, wts["wua"], wts["wub"],
                   wts["wo"], wts["g2"], wts["w1"], wts["w2"], DENSE_TILE, FF_CHUNK)
    return y.reshape(bsz, t, D_MODEL), shift_new, _pairs_to_state(s_pairs), k_new, v_new


def _prep_weights(g_norm1, w_in, rwkv_mu, rwkv_w0, rwkv_w2, rwkv_a0, rwkv_a2, rwkv_g2, rwkv_k_k,
                  rwkv_k_a, rwkv_r_k, rwkv_lnx_w, rwkv_lnx_b, sb_q_norm_g, sb_k_norm_g, w_up_a,
                  w_up_b, w_out, g_norm2, w_ff1, w_ff2):
    row = lambda a: a.reshape(1, -1).astype(F32)
    w_all = jnp.concatenate([_pad_rwkv_cols(w_in[:, :D_RWKV_IN]), w_in[:, D_RWKV_IN:]], axis=1)
    pad_rows = lambda a, rows: jnp.pad(a, ((0, rows - a.shape[0]), (0, 0))).astype(BF16)
    rwkv = (row(_pad_rwkv_cols(rwkv_mu)), row(rwkv_w0), pad_rows(rwkv_w2, LANES), row(rwkv_a0),
            pad_rows(rwkv_a2, LANES), pad_rows(rwkv_g2, 2 * LANES), row(rwkv_k_k), row(rwkv_k_a),
            row(rwkv_r_k), row(rwkv_lnx_w), row(rwkv_lnx_b))
    return {
        "g1": row(g_norm1), "w_all": w_all.astype(BF16),
        "gq": row(jnp.tile(sb_q_norm_g, N_HEADS)), "gk": row(jnp.tile(sb_k_norm_g, N_HEADS)),
        "rwkv": rwkv,
        "wua": w_up_a.astype(BF16), "wub": w_up_b.astype(BF16), "wo": w_out.astype(BF16),
        "g2": row(g_norm2), "w1": w_ff1.astype(BF16), "w2": w_ff2.astype(BF16),
    }


def _constants(t_new):
    hd = jnp.arange(2 * LANES) // HEAD_DIM
    ch = jnp.arange(RWKV_ROWS) // CHUNK
    return {
        "bd": (hd[:, None] == hd[None, :]).astype(BF16),
        "tri_chunks": _tri_ge(RWKV_ROWS) * (ch[:, None] == ch[None, :]).astype(BF16),
        "tri": _tri_ge(TILE),
        "tri_new": _tri_ge(t_new),
    }


def kernel(x_prompt, x_sample, cache_sb_k, cache_sb_v, state_rwkv_wkv, state_rwkv_shift, g_norm1, w_in, rwkv_mu, rwkv_w0, rwkv_w2, rwkv_a0, rwkv_a2, rwkv_g2, rwkv_k_k, rwkv_k_a, rwkv_r_k, rwkv_lnx_w, rwkv_lnx_b, sb_q_norm_g, sb_k_norm_g, w_up_a, w_up_b, w_out, g_norm2, w_ff1, w_ff2):
    depth = w_in.shape[0]
    assert depth == 1
    layer_w = (g_norm1, w_in, rwkv_mu, rwkv_w0, rwkv_w2, rwkv_a0, rwkv_a2, rwkv_g2, rwkv_k_k,
               rwkv_k_a, rwkv_r_k, rwkv_lnx_w, rwkv_lnx_b, sb_q_norm_g, sb_k_norm_g, w_up_a,
               w_up_b, w_out, g_norm2, w_ff1, w_ff2)
    wts = _prep_weights(*(a[0] for a in layer_w))
    consts = _constants(x_sample.shape[1])
    bp = x_prompt.shape[0]
    s0_p = jnp.zeros((bp, N_HEADS, HEAD_DIM, HEAD_DIM), F32)
    shift0_p = jnp.zeros((bp, 1, D_RWKV_IN), x_prompt.dtype)
    y_p, sh_p, s_p, k_p, v_p = _layer(x_prompt, shift0_p, s0_p, None, None, wts, consts)
    y_s, sh_s, s_s, k_s, v_s = _layer(x_sample, state_rwkv_shift[0], state_rwkv_wkv[0],
                                      jnp.swapaxes(cache_sb_k[0], 2, 3),
                                      jnp.swapaxes(cache_sb_v[0], 2, 3), wts, consts)
    return (y_p, y_s, k_p[None], v_p[None], s_p[None], sh_p[None],
            k_s[None], v_s[None], s_s[None], sh_s[None])
```

```python
import functools

import jax
import jax.numpy as jnp
import numpy as np
from jax import lax
from jax.experimental import pallas as pl
from jax.experimental.pallas import tpu as pltpu

F32 = jnp.float32
BF16 = jnp.bfloat16

D_MODEL = 1024
HEAD_DIM = 64
D_RWKV = 512
D_SB = 512
N_HEADS = 8
N_PAIRS = 4
LANES = 128
SUBLANES = 8
D_DECAY_LORA = 64
D_AAA_LORA = 64
D_GATE_LORA = 160
D_RWKV_IN = 3 * D_RWKV + D_DECAY_LORA + D_AAA_LORA + D_GATE_LORA
D_FF = 4 * D_MODEL
CHUNK = 64
RWKV_ROWS = 256
TILE = 256
SAMPLE_SEQS = 2
DENSE_TILE = 512
FF_CHUNK = 1024
SB_SCALE = HEAD_DIM ** -0.5
LOG2E = 1.4426950408889634
Q_SCALE = SB_SCALE * LOG2E
SKIP_BITS = 150.0
RMS_EPS = 1e-6
GN_EPS = 64e-5
L2_EPS = 1e-24

PW_OFF = 3 * D_RWKV
PA_OFF = PW_OFF + LANES
PG_OFF = PA_OFF + LANES
D_RWKV_PAD = PG_OFF + 2 * LANES
SB_OFF = D_RWKV_PAD
GATE_OFF = SB_OFF + 3 * D_SB

VMEM_LIMIT = 56 * 1024 * 1024


def _dot(a, b):
    return jnp.dot(a, b, preferred_element_type=F32)


def _dot_nt(a, b):
    return lax.dot_general(a, b, (((1,), (1,)), ((), ())), preferred_element_type=F32)


def _split(x):
    hi = x.astype(BF16)
    lo = (x - hi.astype(F32)).astype(BF16)
    return hi, lo


def _dot_lh(m, x):
    hi, lo = _split(x)
    return _dot(m, hi) + _dot(m, lo)


def _head_sum(x, bd):
    xb = x.astype(BF16)
    w = bd.shape[0]
    return jnp.concatenate([_dot(xb[:, c * w:(c + 1) * w], bd) for c in range(x.shape[1] // w)],
                           axis=1)


def _mm(a, b):
    return _dot(a.astype(BF16), b.astype(BF16))


def _mm_nt(a, b):
    return _dot_nt(a.astype(BF16), b.astype(BF16))


def _softplus(z):
    return jnp.maximum(z, 0.0) + jnp.log(1.0 + jnp.exp(-jnp.abs(z)))


def _softplus2(z):
    return jnp.maximum(z, 0.0) + jnp.log2(1.0 + jnp.exp2(-jnp.abs(z)))


def _sigmoid(z):
    return 0.5 * jnp.tanh(0.5 * z) + 0.5


def _inproj_kernel(x_ref, g1_ref, w_ref, gq_ref, gk_ref, bd_ref,
                   pr_ref, q_ref, kh_ref, vh_ref, kb_ref, vb_ref, gate_ref):
    x = x_ref[0]
    ms = jnp.mean(x * x, axis=-1, keepdims=True)
    xn = (x * lax.rsqrt(ms + RMS_EPS) * g1_ref[...]).astype(BF16)
    pr_ref[0] = _dot(xn, w_ref[:, 0:D_RWKV_PAD])
    bd = bd_ref[...]

    def head_norm(u, g):
        ss = _head_sum(u * u, bd) * (1.0 / HEAD_DIM)
        return u * lax.rsqrt(ss + RMS_EPS) * g

    q = head_norm(_dot(xn, w_ref[:, SB_OFF:SB_OFF + D_SB]), gq_ref[...])
    k = head_norm(_dot(xn, w_ref[:, SB_OFF + D_SB:SB_OFF + 2 * D_SB]), gk_ref[...])
    v = _dot(xn, w_ref[:, SB_OFF + 2 * D_SB:SB_OFF + 3 * D_SB])
    q_ref[0] = (q * Q_SCALE).astype(BF16)
    kb_ref[0] = k.astype(BF16)
    vb_ref[0] = v.astype(BF16)
    kh_ref[0] = k.T.reshape(N_HEADS, HEAD_DIM, k.shape[0])
    vh_ref[0] = v.T.reshape(N_HEADS, HEAD_DIM, v.shape[0])
    gate_ref[0] = _sigmoid(_dot(xn, w_ref[:, GATE_OFF:GATE_OFF + 2 * D_MODEL]))


def _inproj(x, g1, w_all, gq, gk, bd, tm):
    b, t, _ = x.shape
    row = lambda w: pl.BlockSpec((1, tm, w), lambda i, j: (i, j, 0))
    heads = pl.BlockSpec((1, N_HEADS, HEAD_DIM, tm), lambda i, j: (i, 0, 0, j))
    const = lambda a: pl.BlockSpec(a.shape, lambda i, j: (0, 0), pipeline_mode=pl.Buffered(1))
    return pl.pallas_call(
        _inproj_kernel,
        grid=(b, t // tm),
        in_specs=[row(D_MODEL), const(g1), const(w_all), const(gq), const(gk), const(bd)],
        out_specs=[row(D_RWKV_PAD), row(D_SB), heads, heads, row(D_SB), row(D_SB),
                   row(2 * D_MODEL)],
        out_shape=[jax.ShapeDtypeStruct((b, t, D_RWKV_PAD), F32),
                   jax.ShapeDtypeStruct((b, t, D_SB), BF16),
                   jax.ShapeDtypeStruct((b, N_HEADS, HEAD_DIM, t), F32),
                   jax.ShapeDtypeStruct((b, N_HEADS, HEAD_DIM, t), F32),
                   jax.ShapeDtypeStruct((b, t, D_SB), BF16),
                   jax.ShapeDtypeStruct((b, t, D_SB), BF16),
                   jax.ShapeDtypeStruct((b, t, 2 * D_MODEL), F32)],
        compiler_params=pltpu.CompilerParams(dimension_semantics=("parallel", "parallel"),
                                             vmem_limit_bytes=VMEM_LIMIT),
        name="inproj",
    )(x, g1, w_all, gq, gk, bd)


def _rwkv_kernel(p_ref, shift_ref, s0_ref, mu_ref, w0_ref, w2_ref, a0_ref, a2_ref, g2_ref,
                 kk_ref, ka_ref, rk_ref, lw_ref, lb_ref, bd_ref, tri_ref,
                 y_ref, sout_ref, carry_sc, s_sc, *, bb, tt):
    c = pl.program_id(1)
    rows = bb * tt

    @pl.when(c == 0)
    def _():
        carry_sc[...] = shift_ref[...]
        s_sc[...] = s0_ref[...]

    p = p_ref[...].reshape(rows, D_RWKV_PAD)
    rolled = pltpu.roll(p, 1, 0)
    first_row = lax.broadcasted_iota(jnp.int32, (SUBLANES, D_RWKV_PAD), 0) == 0
    pieces = []
    for b_ in range(bb):
        lo = b_ * tt
        pieces += [jnp.where(first_row, carry_sc[b_], rolled[lo:lo + SUBLANES]),
                   rolled[lo + SUBLANES:lo + tt]]
        carry_sc[b_] = p[lo + tt - 1:lo + tt, :]
    prev = jnp.concatenate(pieces, axis=0)
    pm = p + (prev - p) * mu_ref[...]
    r = pm[:, 0:D_RWKV]
    k = pm[:, D_RWKV:2 * D_RWKV]
    v = pm[:, 2 * D_RWKV:3 * D_RWKV]
    wl = pm[:, PW_OFF:PW_OFF + LANES]
    al = pm[:, PA_OFF:PA_OFF + LANES]
    gl = pm[:, PG_OFF:PG_OFF + 2 * LANES]

    bd = bd_ref[...]
    w = -_softplus(-(w0_ref[...] + _mm(jnp.tanh(wl), w2_ref[...]))) - 0.5
    ld = -jnp.exp(w)
    a = _sigmoid(a0_ref[...] + _mm(al, a2_ref[...]))
    g = _mm(_sigmoid(gl), g2_ref[...])
    kk = k * kk_ref[...]
    kk = kk * lax.rsqrt(jnp.maximum(_head_sum(kk * kk, bd), L2_EPS))
    k = k * (1.0 + (a - 1.0) * ka_ref[...])
    bonus = _head_sum(r * k * rk_ref[...], bd) * v

    n_ch = tt // CHUNK
    chunks = [(b_, ch) for b_ in range(bb) for ch in range(n_ch)]
    rows_of = lambda b_, ch: slice(b_ * tt + ch * CHUNK, b_ * tt + (ch + 1) * CHUNK)
    cl = _dot_lh(tri_ref[...], ld)
    cl_end = jnp.concatenate(
        [jnp.broadcast_to(cl[rows_of(*c).stop - 1:rows_of(*c).stop, :], (CHUNK, D_RWKV))
         for c in chunks], axis=0)
    w_inv = jnp.exp(-cl)
    w_end = jnp.exp(cl_end - cl)
    at = -kk * jnp.exp(cl - ld)
    bt = kk * a * w_inv
    kt = k * w_inv
    rt = r * jnp.exp(cl)
    be = kk * a * w_end
    ke = k * w_end
    wc = jnp.exp(cl_end)

    lane = lax.broadcasted_iota(jnp.int32, (CHUNK, LANES), 1)
    m0 = lane < HEAD_DIM
    ri = lax.broadcasted_iota(jnp.int32, (LANES, LANES), 0)
    ci = lax.broadcasted_iota(jnp.int32, (LANES, LANES), 1)
    same = (ri // CHUNK) == (ci // CHUNK)
    strict = same & ((ri % CHUNK) > (ci % CHUNK))
    incl = same & ((ri % CHUNK) >= (ci % CHUNK))
    eye = (ri == ci).astype(F32)

    def stack(u):
        zero = jnp.zeros_like(u)
        return jnp.concatenate([jnp.where(m0, u, zero), jnp.where(m0, zero, u)], axis=0)

    stack_b = lambda u: stack(u.astype(BF16))

    keys = [(b_, ch, pr) for (b_, ch) in chunks for pr in range(N_PAIRS)]
    blk = lambda x, key: x[rows_of(key[0], key[1]), key[2] * LANES:(key[2] + 1) * LANES]
    la = {q: stack_b(blk(at, q)) for q in keys}
    lr = {q: stack(blk(rt, q)) for q in keys}
    vst = {q: stack(blk(v, q)) for q in keys}
    twice = lambda u: jnp.concatenate([u, u], axis=0)
    g4 = {q: _mm_nt(jnp.concatenate([la[q], lr[q].astype(BF16)], axis=0),
                    jnp.concatenate([twice(blk(bt, q)), twice(blk(kt, q))], axis=0)) for q in keys}
    a_ab = {q: jnp.where(strict, g4[q][:LANES, :LANES], 0.0) for q in keys}
    a_ak = {q: jnp.where(strict, g4[q][:LANES, LANES:], 0.0) for q in keys}
    a_rb = {q: jnp.where(incl, g4[q][LANES:, :LANES], 0.0) for q in keys}
    a_rk = {q: jnp.where(incl, g4[q][LANES:, LANES:], 0.0) for q in keys}
    tinv = {q: eye + a_ab[q] for q in keys}
    apow = {q: _mm(a_ab[q], a_ab[q]) for q in keys}
    for _ in range(4):
        nxt = {q: _mm(apow[q], jnp.concatenate([apow[q], tinv[q]], axis=1)) for q in keys}
        apow = {q: nxt[q][:, :LANES] for q in keys}
        tinv = {q: tinv[q] + nxt[q][:, LANES:] for q in keys}
    tinv = {q: tinv[q] + _mm(apow[q], tinv[q]) for q in keys}
    akv = {q: _mm(jnp.concatenate([a_ak[q], a_rk[q]], axis=0), vst[q]) for q in keys}
    tlav = {q: _mm(tinv[q], jnp.concatenate([la[q], akv[q][:LANES].astype(BF16)], axis=1))
            for q in keys}
    rlon = {q: _mm(a_rb[q], tlav[q]) for q in keys}
    rl = {q: lr[q] + rlon[q][:, :LANES] for q in keys}
    on = {q: rlon[q][:, LANES:] + akv[q][LANES:] for q in keys}
    mn = {q: _mm(tlav[q].T, stack_b(blk(be, q))) for q in keys}
    nc = {q: mn[q][LANES:] + _mm(vst[q].T, stack_b(blk(ke, q))) for q in keys}

    out_rows = []
    for b_ in range(bb):
        state = [s_sc[b_, pr] for pr in range(N_PAIRS)]
        for ch in range(n_ch):
            outs = []
            for pr in range(N_PAIRS):
                q = (b_, ch, pr)
                s_b = state[pr].astype(BF16)
                o_st = _dot_nt(rl[q].astype(BF16), s_b) + on[q]
                outs.append(o_st[0:CHUNK] + o_st[CHUNK:2 * CHUNK])
                state[pr] = (state[pr] * blk(wc, q)[0:1, :] + _dot(s_b, mn[q][:LANES].astype(BF16))
                             + nc[q])
            out_rows.append(jnp.concatenate(outs, axis=1))
        for pr in range(N_PAIRS):
            s_sc[b_, pr] = state[pr]
            sout_ref[b_, pr] = state[pr]
    o = jnp.concatenate(out_rows, axis=0)

    mean = _head_sum(o, bd) * (1.0 / HEAD_DIM)
    oc = o - mean
    var = _head_sum(oc * oc, bd) * (1.0 / HEAD_DIM)
    o = oc * lax.rsqrt(var + GN_EPS)
    y_ref[...] = ((o * lw_ref[...] + lb_ref[...] + bonus) * g).reshape(bb, tt, D_RWKV)


def _rwkv(p3d, shift_pad, s0_bd, params, bd, tri, bb, tt):
    b, t, _ = p3d.shape
    const = lambda a: pl.BlockSpec(a.shape, lambda i, j: (0,) * a.ndim)
    state = pl.BlockSpec((bb, N_PAIRS, LANES, LANES), lambda i, j: (i, 0, 0, 0))
    return pl.pallas_call(
        functools.partial(_rwkv_kernel, bb=bb, tt=tt),
        grid=(b // bb, t // tt),
        in_specs=[pl.BlockSpec((bb, tt, D_RWKV_PAD), lambda i, j: (i, j, 0)),
                  pl.BlockSpec((bb, 1, D_RWKV_PAD), lambda i, j: (i, 0, 0)), state]
                 + [const(a) for a in params] + [const(bd), const(tri)],
        out_specs=[pl.BlockSpec((bb, tt, D_RWKV), lambda i, j: (i, j, 0)), state],
        out_shape=[jax.ShapeDtypeStruct((b, t, D_RWKV), F32),
                   jax.ShapeDtypeStruct((b, N_PAIRS, LANES, LANES), F32)],
        scratch_shapes=[pltpu.VMEM((bb, 1, D_RWKV_PAD), F32),
                        pltpu.VMEM((bb, N_PAIRS, LANES, LANES), F32)],
        compiler_params=pltpu.CompilerParams(dimension_semantics=("parallel", "arbitrary"),
                                             vmem_limit_bytes=VMEM_LIMIT),
        name="rwkv",
    )(p3d, shift_pad, s0_bd, *params, bd, tri)


def _sb_prompt_kernel(q_ref, k_ref, v_ref, tri_ref, y_ref, qm_sc, acc_sc, carry_sc, *, tq):
    i = pl.program_id(1)
    half = tq // 2
    lane_q = lax.broadcasted_iota(jnp.int32, (tq, LANES), 1) < HEAD_DIM
    causal = (lax.broadcasted_iota(jnp.int32, (half, half), 1)
              < lax.broadcasted_iota(jnp.int32, (half, half), 0))
    for pr in range(N_PAIRS):
        qp = q_ref[0, :, pr * LANES:(pr + 1) * LANES]
        qm_sc[2 * pr] = jnp.where(lane_q, qp, jnp.zeros_like(qp))
        qm_sc[2 * pr + 1] = jnp.where(lane_q, jnp.zeros_like(qp), qp)

    heads = range(N_HEADS)
    k_pair = lambda h, start, n: k_ref[0, pl.ds(start, n), (h // 2) * LANES:(h // 2 + 1) * LANES]

    def set_mass(cols):
        mass = cols[0]
        for h in heads:
            carry_sc[h] = jnp.broadcast_to(cols[h], (tq, LANES))
            mass = jnp.minimum(mass, cols[h])
        return jnp.min(mass)

    def pv_pair(pr, att, start, n_keys):
        vblk = v_ref[0, pl.ds(start, n_keys), pr * LANES:(pr + 1) * LANES]
        lo = lax.broadcasted_iota(jnp.int32, (n_keys, LANES), 1) < HEAD_DIM
        v2 = jnp.concatenate([jnp.where(lo, vblk, jnp.zeros_like(vblk)),
                              jnp.where(lo, jnp.zeros_like(vblk), vblk)], axis=0)
        att2 = jnp.concatenate([att[2 * pr].astype(BF16), att[2 * pr + 1].astype(BF16)], axis=1)
        return _dot(att2, v2)

    def block(j):
        start = pl.multiple_of(j * tq, tq)
        tri = tri_ref[...]
        z = [_dot_nt(qm_sc[h], k_pair(h, start, tq)) for h in heads]
        sp = [_softplus2(z[h]) for h in heads]
        tail = [_dot(sp[h].astype(BF16), tri)
                + jnp.concatenate([carry_sc[h]] * (tq // LANES), axis=1) for h in heads]
        att = [jnp.exp2(z[h] - tail[h]) for h in heads]
        for pr in range(N_PAIRS):
            acc_sc[pr] += pv_pair(pr, att, start, tq)
        return set_mass([t_[:, 0:1] for t_ in tail])

    def first_blocks(with_prev):
        start = pl.multiple_of(i * tq, tq)
        prev = pl.multiple_of(jnp.maximum(i - 1, 0) * tq, tq)
        tri = tri_ref[...]
        keep = lambda u: jnp.where(causal, u, 0.0)
        right = lambda u: jnp.concatenate([u[:, :half], keep(u[:, half:])], axis=1)
        z_t = [_dot_nt(qm_sc[h, :half], k_pair(h, start, half)) for h in heads]
        z_b = [_dot_nt(qm_sc[h, half:], k_pair(h, start, tq)) for h in heads]
        z_p = [_dot_nt(qm_sc[h], k_pair(h, prev, tq)) for h in heads] if with_prev else []
        sp_t = [keep(_softplus2(z)) for z in z_t]
        sp_b = [right(_softplus2(z)) for z in z_b]
        sp_p = [_softplus2(z) for z in z_p]
        tail_t = [_dot(s.astype(BF16), tri[:half, :half]) for s in sp_t]
        tail_b = [_dot(s.astype(BF16), tri) for s in sp_b]
        tail_p = [_dot(s.astype(BF16), tri) for s in sp_p]
        att_t = [keep(jnp.exp2(z_t[h] - tail_t[h])) for h in heads]
        att_b = [right(jnp.exp2(z_b[h] - tail_b[h])) for h in heads]
        cols = [jnp.concatenate([tail_t[h][:, 0:1], tail_b[h][:, 0:1]], axis=0) for h in heads]
        if with_prev:
            tail_p = [tail_p[h] + cols[h] for h in heads]
            att_p = [jnp.exp2(z_p[h] - tail_p[h]) for h in heads]
            cols = [t_[:, 0:1] for t_ in tail_p]
        for pr in range(N_PAIRS):
            pv = jnp.concatenate([pv_pair(pr, att_t, start, half), pv_pair(pr, att_b, start, tq)], axis=0)
            acc_sc[pr] = pv + pv_pair(pr, att_p, prev, tq) if with_prev else pv
        return set_mass(cols)

    @pl.when(i == 0)
    def _():
        first_blocks(False)

    @pl.when(i > 0)
    def _():
        def body(state):
            jj, _ = state
            return jj + 1, block(i - 2 - jj)

        lax.while_loop(lambda s: jnp.logical_and(s[0] < i - 1, s[1] < SKIP_BITS), body,
                       (jnp.int32(0), first_blocks(True)))

    for pr in range(N_PAIRS):
        y_ref[0, :, pr * LANES:(pr + 1) * LANES] = acc_sc[pr]


def _sb_prompt(qb, kb, vb, tri, tq):
    b, t, _ = qb.shape
    full = pl.BlockSpec((1, t, D_SB), lambda i, j: (i, 0, 0))
    tile = pl.BlockSpec((1, tq, D_SB), lambda i, j: (i, j, 0))
    return pl.pallas_call(
        functools.partial(_sb_prompt_kernel, tq=tq),
        grid=(b, t // tq),
        in_specs=[tile, full, full, pl.BlockSpec(tri.shape, lambda i, j: (0, 0))],
        out_specs=tile,
        out_shape=jax.ShapeDtypeStruct((b, t, D_SB), F32),
        scratch_shapes=[pltpu.VMEM((N_HEADS, tq, LANES), BF16),
                        pltpu.VMEM((N_PAIRS, tq, LANES), F32),
                        pltpu.VMEM((N_HEADS, tq, LANES), F32)],
        compiler_params=pltpu.CompilerParams(dimension_semantics=("parallel", "arbitrary"),
                                             vmem_limit_bytes=VMEM_LIMIT),
        name="sb_prompt",
    )(qb, kb, vb, tri)


def _sb_sample_kernel(q_ref, kn_ref, vn_ref, kct_ref, vct_ref, tri_ref, trin_ref, y_ref):
    t = q_ref.shape[1]
    past = kct_ref.shape[3]
    tri, trin = tri_ref[...], trin_ref[...]
    tk = tri.shape[0]
    nb = past // tk
    causal = (lax.broadcasted_iota(jnp.int32, (t, t), 1) < lax.broadcasted_iota(jnp.int32, (t, t), 0))
    lane_lo = lax.broadcasted_iota(jnp.int32, (t, LANES), 1) < HEAD_DIM
    row_lo = lax.broadcasted_iota(jnp.int32, (LANES, past), 0) < HEAD_DIM
    units = [(s, h) for s in range(q_ref.shape[0]) for h in range(N_HEADS)]
    pair = lambda ref, u: ref[u[0], :, (u[1] // 2) * LANES:(u[1] // 2 + 1) * LANES]
    own = lambda u, x, mask: jnp.where(mask if u[1] % 2 == 0 else jnp.logical_not(mask), x,
                                       jnp.zeros_like(x))
    cache = lambda ref, u: ref[u[0], 2 * (u[1] // 2):2 * (u[1] // 2) + 2].reshape(LANES, past).astype(BF16)
    qm = {u: own(u, pair(q_ref, u), lane_lo) for u in units}
    zn = {u: _dot_nt(qm[u], pair(kn_ref, u)) for u in units}
    zp = {u: _dot(qm[u], cache(kct_ref, u)) for u in units}
    spn = {u: jnp.where(causal, _softplus2(zn[u]), 0.0) for u in units}
    spp = {u: _softplus2(zp[u]) for u in units}
    tailn = {u: _dot(spn[u].astype(BF16), trin) for u in units}
    tailp = {u: [_dot(spp[u][:, c * tk:(c + 1) * tk].astype(BF16), tri) for c in range(nb)]
             for u in units}
    for u in units:
        off = jnp.broadcast_to(tailn[u][:, 0:1], (t, tk))
        for c in reversed(range(nb)):
            tailp[u][c] = tailp[u][c] + off
            off = jnp.broadcast_to(tailp[u][c][:, 0:1], (t, tk))
    out = {}
    for u in units:
        attn = jnp.where(causal, jnp.exp2(zn[u] - tailn[u]), 0.0).astype(BF16)
        attp = jnp.exp2(zp[u] - jnp.concatenate(tailp[u], axis=1)).astype(BF16)
        yh = (_dot(attn, own(u, pair(vn_ref, u), lane_lo))
              + _dot_nt(attp, own(u, cache(vct_ref, u), row_lo)))
        key = (u[0], u[1] // 2)
        out[key] = yh if key not in out else out[key] + yh
    for (s, pr), val in out.items():
        y_ref[s, :, pr * LANES:(pr + 1) * LANES] = val


def _sb_sample(qb, kb, vb, kct, vct, tri, trin):
    b, t, _ = qb.shape
    past = kct.shape[3]
    ns = SAMPLE_SEQS
    new = pl.BlockSpec((ns, t, D_SB), lambda i: (i, 0, 0))
    old = pl.BlockSpec((ns, N_HEADS, HEAD_DIM, past), lambda i: (i, 0, 0, 0))
    const = lambda a: pl.BlockSpec(a.shape, lambda i: (0, 0))
    return pl.pallas_call(
        _sb_sample_kernel,
        grid=(b // ns,),
        in_specs=[new, new, new, old, old, const(tri), const(trin)],
        out_specs=new,
        out_shape=jax.ShapeDtypeStruct((b, t, D_SB), F32),
        compiler_params=pltpu.CompilerParams(dimension_semantics=("parallel",),
                                             vmem_limit_bytes=VMEM_LIMIT),
        name="sb_sample",
    )(qb, kb, vb, kct, vct, tri, trin)


def _merge_ffn_kernel(x_ref, ya_ref, yb_ref, gate_ref, wua_ref, wub_ref, wo_ref, g2_ref,
                      w1_ref, w2_ref, y_ref, *, ff_chunk):
    gate = gate_ref[...]
    merged = (gate[:, 0:D_MODEL] * _dot(ya_ref[...].astype(BF16), wua_ref[...])
              + gate[:, D_MODEL:2 * D_MODEL] * _dot(yb_ref[...].astype(BF16), wub_ref[...]))
    x = x_ref[...] + _dot(merged.astype(BF16), wo_ref[...])
    ms = jnp.mean(x * x, axis=-1, keepdims=True)
    xn = (x * lax.rsqrt(ms + RMS_EPS) * g2_ref[...]).astype(BF16)
    for c in range(D_FF // ff_chunk):
        sl = slice(c * ff_chunk, (c + 1) * ff_chunk)
        h = jnp.maximum(_dot(xn, w1_ref[:, sl]), 0.0)
        x = x + _dot((h * h).astype(BF16), w2_ref[sl, :])
    y_ref[...] = x


def _merge_ffn(x2d, ya, yb, gate, wua, wub, wo, g2, w1, w2, tm, ff_chunk):
    n = x2d.shape[0]
    row = lambda w: pl.BlockSpec((tm, w), lambda i: (i, 0))
    const = lambda a: pl.BlockSpec(a.shape, lambda i: (0, 0), pipeline_mode=pl.Buffered(1))
    return pl.pallas_call(
        functools.partial(_merge_ffn_kernel, ff_chunk=ff_chunk),
        grid=(n // tm,),
        in_specs=[row(D_MODEL), row(D_RWKV), row(D_SB), row(2 * D_MODEL),
                  const(wua), const(wub), const(wo), const(g2), const(w1), const(w2)],
        out_specs=row(D_MODEL),
        out_shape=jax.ShapeDtypeStruct((n, D_MODEL), F32),
        compiler_params=pltpu.CompilerParams(dimension_semantics=("parallel",),
                                             vmem_limit_bytes=VMEM_LIMIT),
        name="merge_ffn",
    )(x2d, ya, yb, gate, wua, wub, wo, g2, w1, w2)


def _pad_cols(a, width):
    return jnp.pad(a, [(0, 0)] * (a.ndim - 1) + [(0, width - a.shape[-1])])


def _pad_rwkv_cols(a):
    o1 = 3 * D_RWKV + D_DECAY_LORA
    o2 = o1 + D_AAA_LORA
    return jnp.concatenate([a[..., :3 * D_RWKV],
                            _pad_cols(a[..., 3 * D_RWKV:o1], LANES),
                            _pad_cols(a[..., o1:o2], LANES),
                            _pad_cols(a[..., o2:], 2 * LANES)], axis=-1)


def _unpad_rwkv_cols(a):
    return jnp.concatenate([a[..., :PW_OFF + D_DECAY_LORA],
                            a[..., PA_OFF:PA_OFF + D_AAA_LORA],
                            a[..., PG_OFF:PG_OFF + D_GATE_LORA]], axis=-1)


def _state_to_pairs(s):
    b = s.shape[0]
    s = s.reshape(b, N_PAIRS, 2, HEAD_DIM, HEAD_DIM)
    z = jnp.zeros_like(s[:, :, 0])
    top = jnp.concatenate([s[:, :, 0], z], axis=-1)
    bot = jnp.concatenate([z, s[:, :, 1]], axis=-1)
    return jnp.concatenate([top, bot], axis=-2)


def _pairs_to_state(sp):
    b = sp.shape[0]
    h0 = sp[:, :, :HEAD_DIM, :HEAD_DIM]
    h1 = sp[:, :, HEAD_DIM:, HEAD_DIM:]
    return jnp.stack([h0, h1], axis=2).reshape(b, N_HEADS, HEAD_DIM, HEAD_DIM)


def _tri_ge(n):
    i = np.arange(n)
    return i[:, None] >= i[None, :]


def _layer(x, shift_prev, s0, kt_past, vt_past, wts, consts):
    bsz, t, _ = x.shape
    bb = min(bsz, RWKV_ROWS // CHUNK)
    tt = RWKV_ROWS // bb
    n = bsz * t
    x2d = x.reshape(n, D_MODEL)
    flat = t % DENSE_TILE != 0
    pr3, qb, kt, vt, kb, vb, gate = _inproj(x2d[None] if flat else x, wts["g1"], wts["w_all"],
                                            wts["gq"], wts["gk"], consts["bd"], DENSE_TILE)
    if flat:
        pr3 = pr3.reshape(bsz, t, D_RWKV_PAD)
        qb, kb, vb = (a.reshape(bsz, t, D_SB) for a in (qb, kb, vb))
        unflat = lambda a: a.reshape(N_HEADS, HEAD_DIM, bsz, t).transpose(2, 0, 3, 1)
        k_new, v_new = unflat(kt), unflat(vt)
    else:
        k_new, v_new = jnp.swapaxes(kt, 2, 3), jnp.swapaxes(vt, 2, 3)
    gate = gate.reshape(n, 2 * D_MODEL)
    shift_new = _unpad_rwkv_cols(pr3[:, t - 1:t, :])
    ya, s_pairs = _rwkv(pr3, _pad_rwkv_cols(shift_prev), _state_to_pairs(s0.astype(F32)),
                        wts["rwkv"], consts["bd"], consts["tri_chunks"], bb, tt)
    if kt_past is None:
        yb = _sb_prompt(qb, kb, vb, consts["tri"], TILE)
    else:
        yb = _sb_sample(qb, kb, vb, kt_past, vt_past, consts["tri"], consts["tri_new"])
    y = _merge_ffn(x2d, ya.reshape(n, D_RWKV), yb.reshape(n, D_SB), gate, wts["wua"], wts["wub"],
                   wts["wo"], wts["g2"], wts["w1"], wts["w2"], DENSE_TILE, FF_CHUNK)
    return y.reshape(bsz, t, D_MODEL), shift_new, _pairs_to_state(s_pairs), k_new, v_new


def _prep_weights(g_norm1, w_in, rwkv_mu, rwkv_w0, rwkv_w2, rwkv_a0, rwkv_a2, rwkv_g2, rwkv_k_k,
                  rwkv_k_a, rwkv_r_k, rwkv_lnx_w, rwkv_lnx_b, sb_q_norm_g, sb_k_norm_g, w_up_a,
                  w_up_b, w_out, g_norm2, w_ff1, w_ff2):
    row = lambda a: a.reshape(1, -1).astype(F32)
    w_all = jnp.concatenate([_pad_rwkv_cols(w_in[:, :D_RWKV_IN]), w_in[:, D_RWKV_IN:]], axis=1)
    pad_rows = lambda a, rows: jnp.pad(a, ((0, rows - a.shape[0]), (0, 0))).astype(BF16)
    rwkv = (row(_pad_rwkv_cols(rwkv_mu)), row(rwkv_w0), pad_rows(rwkv_w2, LANES), row(rwkv_a0),
            pad_rows(rwkv_a2, LANES), pad_rows(rwkv_g2, 2 * LANES), row(rwkv_k_k), row(rwkv_k_a),
            row(rwkv_r_k), row(rwkv_lnx_w), row(rwkv_lnx_b))
    return {
        "g1": row(g_norm1), "w_all": w_all.astype(BF16),
        "gq": row(jnp.tile(sb_q_norm_g, N_HEADS)), "gk": row(jnp.tile(sb_k_norm_g, N_HEADS)),
        "rwkv": rwkv,
        "wua": w_up_a.astype(BF16), "wub": w_up_b.astype(BF16), "wo": w_out.astype(BF16),
        "g2": row(g_norm2), "w1": w_ff1.astype(BF16), "w2": w_ff2.astype(BF16),
    }


def _constants(t_new):
    hd = np.arange(2 * LANES) // HEAD_DIM
    ch = np.arange(RWKV_ROWS) // CHUNK
    mats = {
        "bd": hd[:, None] == hd[None, :],
        "tri_chunks": _tri_ge(RWKV_ROWS) & (ch[:, None] == ch[None, :]),
        "tri": _tri_ge(TILE),
        "tri_new": _tri_ge(t_new),
    }
    return {name: jnp.asarray(m.astype(np.float32), BF16) for name, m in mats.items()}


def kernel(x_prompt, x_sample, cache_sb_k, cache_sb_v, state_rwkv_wkv, state_rwkv_shift, g_norm1, w_in, rwkv_mu, rwkv_w0, rwkv_w2, rwkv_a0, rwkv_a2, rwkv_g2, rwkv_k_k, rwkv_k_a, rwkv_r_k, rwkv_lnx_w, rwkv_lnx_b, sb_q_norm_g, sb_k_norm_g, w_up_a, w_up_b, w_out, g_norm2, w_ff1, w_ff2):
    depth = w_in.shape[0]
    assert depth == 1
    layer_w = (g_norm1, w_in, rwkv_mu, rwkv_w0, rwkv_w2, rwkv_a0, rwkv_a2, rwkv_g2, rwkv_k_k,
               rwkv_k_a, rwkv_r_k, rwkv_lnx_w, rwkv_lnx_b, sb_q_norm_g, sb_k_norm_g, w_up_a,
               w_up_b, w_out, g_norm2, w_ff1, w_ff2)
    wts = _prep_weights(*(a[0] for a in layer_w))
    consts = _constants(x_sample.shape[1])
    bp = x_prompt.shape[0]
    s0_p = jnp.zeros((bp, N_HEADS, HEAD_DIM, HEAD_DIM), F32)
    shift0_p = jnp.zeros((bp, 1, D_RWKV_IN), x_prompt.dtype)
    y_p, sh_p, s_p, k_p, v_p = _layer(x_prompt, shift0_p, s0_p, None, None, wts, consts)
    y_s, sh_s, s_s, k_s, v_s = _layer(x_sample, state_rwkv_shift[0], state_rwkv_wkv[0],
                                      jnp.swapaxes(cache_sb_k[0], 2, 3),
                                      jnp.swapaxes(cache_sb_v[0], 2, 3), wts, consts)
    return (y_p, y_s, k_p[None], v_p[None], s_p[None], sh_p[None],
            k_s[None], v_s[None], s_s[None], sh_s[None])
```

```python
import functools

import jax
import jax.numpy as jnp
import numpy as np
from jax import lax
from jax.experimental import pallas as pl
from jax.experimental.pallas import tpu as pltpu

F32 = jnp.float32
BF16 = jnp.bfloat16

D_MODEL = 1024
HEAD_DIM = 64
D_RWKV = 512
D_SB = 512
N_HEADS = 8
N_PAIRS = 4
LANES = 128
SUBLANES = 8
D_DECAY_LORA = 64
D_AAA_LORA = 64
D_GATE_LORA = 160
D_RWKV_IN = 3 * D_RWKV + D_DECAY_LORA + D_AAA_LORA + D_GATE_LORA
D_FF = 4 * D_MODEL
CHUNK = 64
RWKV_ROWS = 256
TILE = 256
PROMPT_TILES = 2
SAMPLE_SEQS = 2
DENSE_TILE = 512
FF_CHUNK = 1024
SB_SCALE = HEAD_DIM ** -0.5
LOG2E = 1.4426950408889634
Q_SCALE = SB_SCALE * LOG2E
SKIP_BITS = 150.0
RMS_EPS = 1e-6
GN_EPS = 64e-5
L2_EPS = 1e-24

PW_OFF = 3 * D_RWKV
PA_OFF = PW_OFF + LANES
PG_OFF = PA_OFF + LANES
D_RWKV_PAD = PG_OFF + 2 * LANES
SB_OFF = D_RWKV_PAD
GATE_OFF = SB_OFF + 3 * D_SB

VMEM_LIMIT = 56 * 1024 * 1024


def _dot(a, b):
    return jnp.dot(a, b, preferred_element_type=F32)


def _dot_nt(a, b):
    return lax.dot_general(a, b, (((1,), (1,)), ((), ())), preferred_element_type=F32)


def _split(x):
    hi = x.astype(BF16)
    lo = (x - hi.astype(F32)).astype(BF16)
    return hi, lo


def _dot_lh(m, x):
    hi, lo = _split(x)
    return _dot(m, hi) + _dot(m, lo)


def _head_sum(x, bd):
    xb = x.astype(BF16)
    w = bd.shape[0]
    return jnp.concatenate([_dot(xb[:, c * w:(c + 1) * w], bd) for c in range(x.shape[1] // w)],
                           axis=1)


def _mm(a, b):
    return _dot(a.astype(BF16), b.astype(BF16))


def _mm_nt(a, b):
    return _dot_nt(a.astype(BF16), b.astype(BF16))


def _softplus(z):
    return jnp.maximum(z, 0.0) + jnp.log(1.0 + jnp.exp(-jnp.abs(z)))


def _softplus2(z):
    return jnp.maximum(z, 0.0) + jnp.log2(1.0 + jnp.exp2(-jnp.abs(z)))


def _sigmoid(z):
    return 0.5 * jnp.tanh(0.5 * z) + 0.5


def _inproj_kernel(x_ref, g1_ref, w_ref, gq_ref, gk_ref, bd_ref,
                   pr_ref, q_ref, kh_ref, vh_ref, kb_ref, vb_ref, gate_ref):
    x = x_ref[0]
    ms = jnp.mean(x * x, axis=-1, keepdims=True)
    xn = (x * lax.rsqrt(ms + RMS_EPS) * g1_ref[...]).astype(BF16)
    pr_ref[0] = _dot(xn, w_ref[:, 0:D_RWKV_PAD])
    bd = bd_ref[...]

    def head_norm(u, g):
        ss = _head_sum(u * u, bd) * (1.0 / HEAD_DIM)
        return u * lax.rsqrt(ss + RMS_EPS) * g

    q = head_norm(_dot(xn, w_ref[:, SB_OFF:SB_OFF + D_SB]), gq_ref[...])
    k = head_norm(_dot(xn, w_ref[:, SB_OFF + D_SB:SB_OFF + 2 * D_SB]), gk_ref[...])
    v = _dot(xn, w_ref[:, SB_OFF + 2 * D_SB:SB_OFF + 3 * D_SB])
    q_ref[0] = (q * Q_SCALE).astype(BF16)
    kb_ref[0] = k.astype(BF16)
    vb_ref[0] = v.astype(BF16)
    kh_ref[0] = k.T.reshape(N_HEADS, HEAD_DIM, k.shape[0])
    vh_ref[0] = v.T.reshape(N_HEADS, HEAD_DIM, v.shape[0])
    gate_ref[0] = _sigmoid(_dot(xn, w_ref[:, GATE_OFF:GATE_OFF + 2 * D_MODEL]))


def _inproj(x, g1, w_all, gq, gk, bd, tm):
    b, t, _ = x.shape
    row = lambda w: pl.BlockSpec((1, tm, w), lambda i, j: (i, j, 0))
    heads = pl.BlockSpec((1, N_HEADS, HEAD_DIM, tm), lambda i, j: (i, 0, 0, j))
    const = lambda a: pl.BlockSpec(a.shape, lambda i, j: (0, 0), pipeline_mode=pl.Buffered(1))
    return pl.pallas_call(
        _inproj_kernel,
        grid=(b, t // tm),
        in_specs=[row(D_MODEL), const(g1), const(w_all), const(gq), const(gk), const(bd)],
        out_specs=[row(D_RWKV_PAD), row(D_SB), heads, heads, row(D_SB), row(D_SB),
                   row(2 * D_MODEL)],
        out_shape=[jax.ShapeDtypeStruct((b, t, D_RWKV_PAD), F32),
                   jax.ShapeDtypeStruct((b, t, D_SB), BF16),
                   jax.ShapeDtypeStruct((b, N_HEADS, HEAD_DIM, t), F32),
                   jax.ShapeDtypeStruct((b, N_HEADS, HEAD_DIM, t), F32),
                   jax.ShapeDtypeStruct((b, t, D_SB), BF16),
                   jax.ShapeDtypeStruct((b, t, D_SB), BF16),
                   jax.ShapeDtypeStruct((b, t, 2 * D_MODEL), F32)],
        compiler_params=pltpu.CompilerParams(dimension_semantics=("parallel", "parallel"),
                                             vmem_limit_bytes=VMEM_LIMIT),
        name="inproj",
    )(x, g1, w_all, gq, gk, bd)


def _rwkv_kernel(p_ref, shift_ref, s0_ref, mu_ref, w0_ref, w2_ref, a0_ref, a2_ref, g2_ref,
                 kk_ref, ka_ref, rk_ref, lw_ref, lb_ref, bd_ref, tri_ref,
                 y_ref, sout_ref, carry_sc, s_sc, *, bb, tt):
    c = pl.program_id(1)
    rows = bb * tt

    @pl.when(c == 0)
    def _():
        carry_sc[...] = shift_ref[...]
        s_sc[...] = s0_ref[...]

    p = p_ref[...].reshape(rows, D_RWKV_PAD)
    rolled = pltpu.roll(p, 1, 0)
    first_row = lax.broadcasted_iota(jnp.int32, (SUBLANES, D_RWKV_PAD), 0) == 0
    pieces = []
    for b_ in range(bb):
        lo = b_ * tt
        pieces += [jnp.where(first_row, carry_sc[b_], rolled[lo:lo + SUBLANES]),
                   rolled[lo + SUBLANES:lo + tt]]
        carry_sc[b_] = p[lo + tt - 1:lo + tt, :]
    prev = jnp.concatenate(pieces, axis=0)
    pm = p + (prev - p) * mu_ref[...]
    r = pm[:, 0:D_RWKV]
    k = pm[:, D_RWKV:2 * D_RWKV]
    v = pm[:, 2 * D_RWKV:3 * D_RWKV]
    wl = pm[:, PW_OFF:PW_OFF + LANES]
    al = pm[:, PA_OFF:PA_OFF + LANES]
    gl = pm[:, PG_OFF:PG_OFF + 2 * LANES]

    bd = bd_ref[...]
    w = -_softplus(-(w0_ref[...] + _mm(jnp.tanh(wl), w2_ref[...]))) - 0.5
    ld = -jnp.exp(w)
    a = _sigmoid(a0_ref[...] + _mm(al, a2_ref[...]))
    g = _mm(_sigmoid(gl), g2_ref[...])
    kk = k * kk_ref[...]
    kk = kk * lax.rsqrt(jnp.maximum(_head_sum(kk * kk, bd), L2_EPS))
    k = k * (1.0 + (a - 1.0) * ka_ref[...])
    bonus = _head_sum(r * k * rk_ref[...], bd) * v

    n_ch = tt // CHUNK
    chunks = [(b_, ch) for b_ in range(bb) for ch in range(n_ch)]
    rows_of = lambda b_, ch: slice(b_ * tt + ch * CHUNK, b_ * tt + (ch + 1) * CHUNK)
    cl = _dot_lh(tri_ref[...], ld)
    cl_end = jnp.concatenate(
        [jnp.broadcast_to(cl[rows_of(*c).stop - 1:rows_of(*c).stop, :], (CHUNK, D_RWKV))
         for c in chunks], axis=0)
    w_inv = jnp.exp(-cl)
    w_end = jnp.exp(cl_end - cl)
    at = -kk * jnp.exp(cl - ld)
    bt = kk * a * w_inv
    kt = k * w_inv
    rt = r * jnp.exp(cl)
    be = kk * a * w_end
    ke = k * w_end
    wc = jnp.exp(cl_end)

    lane = lax.broadcasted_iota(jnp.int32, (CHUNK, LANES), 1)
    m0 = lane < HEAD_DIM
    ri = lax.broadcasted_iota(jnp.int32, (LANES, LANES), 0)
    ci = lax.broadcasted_iota(jnp.int32, (LANES, LANES), 1)
    same = (ri // CHUNK) == (ci // CHUNK)
    strict = same & ((ri % CHUNK) > (ci % CHUNK))
    incl = same & ((ri % CHUNK) >= (ci % CHUNK))
    eye = (ri == ci).astype(F32)

    def stack(u):
        zero = jnp.zeros_like(u)
        return jnp.concatenate([jnp.where(m0, u, zero), jnp.where(m0, zero, u)], axis=0)

    stack_b = lambda u: stack(u.astype(BF16))

    keys = [(b_, ch, pr) for (b_, ch) in chunks for pr in range(N_PAIRS)]
    blk = lambda x, key: x[rows_of(key[0], key[1]), key[2] * LANES:(key[2] + 1) * LANES]
    la = {q: stack_b(blk(at, q)) for q in keys}
    lr = {q: stack(blk(rt, q)) for q in keys}
    vst = {q: stack(blk(v, q)) for q in keys}
    twice = lambda u: jnp.concatenate([u, u], axis=0)
    g4 = {q: _mm_nt(jnp.concatenate([la[q], lr[q].astype(BF16)], axis=0),
                    jnp.concatenate([twice(blk(bt, q)), twice(blk(kt, q))], axis=0)) for q in keys}
    a_ab = {q: jnp.where(strict, g4[q][:LANES, :LANES], 0.0) for q in keys}
    a_ak = {q: jnp.where(strict, g4[q][:LANES, LANES:], 0.0) for q in keys}
    a_rb = {q: jnp.where(incl, g4[q][LANES:, :LANES], 0.0) for q in keys}
    a_rk = {q: jnp.where(incl, g4[q][LANES:, LANES:], 0.0) for q in keys}
    tinv = {q: eye + a_ab[q] for q in keys}
    apow = {q: _mm(a_ab[q], a_ab[q]) for q in keys}
    for _ in range(4):
        nxt = {q: _mm(apow[q], jnp.concatenate([apow[q], tinv[q]], axis=1)) for q in keys}
        apow = {q: nxt[q][:, :LANES] for q in keys}
        tinv = {q: tinv[q] + nxt[q][:, LANES:] for q in keys}
    tinv = {q: tinv[q] + _mm(apow[q], tinv[q]) for q in keys}
    akv = {q: _mm(jnp.concatenate([a_ak[q], a_rk[q]], axis=0), vst[q]) for q in keys}
    tlav = {q: _mm(tinv[q], jnp.concatenate([la[q], akv[q][:LANES].astype(BF16)], axis=1))
            for q in keys}
    rlon = {q: _mm(a_rb[q], tlav[q]) for q in keys}
    rl = {q: lr[q] + rlon[q][:, :LANES] for q in keys}
    on = {q: rlon[q][:, LANES:] + akv[q][LANES:] for q in keys}
    mn = {q: _mm(tlav[q].T, stack_b(blk(be, q))) for q in keys}
    nc = {q: mn[q][LANES:] + _mm(vst[q].T, stack_b(blk(ke, q))) for q in keys}

    out_rows = []
    for b_ in range(bb):
        state = [s_sc[b_, pr] for pr in range(N_PAIRS)]
        for ch in range(n_ch):
            outs = []
            for pr in range(N_PAIRS):
                q = (b_, ch, pr)
                s_b = state[pr].astype(BF16)
                o_st = _dot_nt(rl[q].astype(BF16), s_b) + on[q]
                outs.append(o_st[0:CHUNK] + o_st[CHUNK:2 * CHUNK])
                state[pr] = (state[pr] * blk(wc, q)[0:1, :] + _dot(s_b, mn[q][:LANES].astype(BF16))
                             + nc[q])
            out_rows.append(jnp.concatenate(outs, axis=1))
        for pr in range(N_PAIRS):
            s_sc[b_, pr] = state[pr]
            sout_ref[b_, pr] = state[pr]
    o = jnp.concatenate(out_rows, axis=0)

    mean = _head_sum(o, bd) * (1.0 / HEAD_DIM)
    oc = o - mean
    var = _head_sum(oc * oc, bd) * (1.0 / HEAD_DIM)
    o = oc * lax.rsqrt(var + GN_EPS)
    y_ref[...] = ((o * lw_ref[...] + lb_ref[...] + bonus) * g).reshape(bb, tt, D_RWKV)


def _rwkv(p3d, shift_pad, s0_bd, params, bd, tri, bb, tt):
    b, t, _ = p3d.shape
    const = lambda a: pl.BlockSpec(a.shape, lambda i, j: (0,) * a.ndim)
    state = pl.BlockSpec((bb, N_PAIRS, LANES, LANES), lambda i, j: (i, 0, 0, 0))
    return pl.pallas_call(
        functools.partial(_rwkv_kernel, bb=bb, tt=tt),
        grid=(b // bb, t // tt),
        in_specs=[pl.BlockSpec((bb, tt, D_RWKV_PAD), lambda i, j: (i, j, 0)),
                  pl.BlockSpec((bb, 1, D_RWKV_PAD), lambda i, j: (i, 0, 0)), state]
                 + [const(a) for a in params] + [const(bd), const(tri)],
        out_specs=[pl.BlockSpec((bb, tt, D_RWKV), lambda i, j: (i, j, 0)), state],
        out_shape=[jax.ShapeDtypeStruct((b, t, D_RWKV), F32),
                   jax.ShapeDtypeStruct((b, N_PAIRS, LANES, LANES), F32)],
        scratch_shapes=[pltpu.VMEM((bb, 1, D_RWKV_PAD), F32),
                        pltpu.VMEM((bb, N_PAIRS, LANES, LANES), F32)],
        compiler_params=pltpu.CompilerParams(dimension_semantics=("parallel", "arbitrary"),
                                             vmem_limit_bytes=VMEM_LIMIT),
        name="rwkv",
    )(p3d, shift_pad, s0_bd, *params, bd, tri)


def _sb_prompt_kernel(q_ref, k_ref, v_ref, tri_ref, y_ref, qm_sc, acc_sc, carry_sc, *, tq, n_sub):
    for s in range(n_sub):
        _sb_prompt_tile(pl.program_id(1) * n_sub + s, slice(s * tq, (s + 1) * tq), s == 0,
                        q_ref, k_ref, v_ref, tri_ref, y_ref, qm_sc, acc_sc, carry_sc, tq)


def _sb_prompt_tile(i, rows, maybe_first, q_ref, k_ref, v_ref, tri_ref, y_ref, qm_sc, acc_sc,
                    carry_sc, tq):
    half = tq // 2
    lane_q = lax.broadcasted_iota(jnp.int32, (tq, LANES), 1) < HEAD_DIM
    causal = (lax.broadcasted_iota(jnp.int32, (half, half), 1)
              < lax.broadcasted_iota(jnp.int32, (half, half), 0))
    for pr in range(N_PAIRS):
        qp = q_ref[0, rows, pr * LANES:(pr + 1) * LANES]
        qm_sc[2 * pr] = jnp.where(lane_q, qp, jnp.zeros_like(qp))
        qm_sc[2 * pr + 1] = jnp.where(lane_q, jnp.zeros_like(qp), qp)

    heads = range(N_HEADS)
    k_pair = lambda h, start, n: k_ref[0, pl.ds(start, n), (h // 2) * LANES:(h // 2 + 1) * LANES]

    def set_mass(cols):
        mass = cols[0]
        for h in heads:
            carry_sc[h] = jnp.broadcast_to(cols[h], (tq, LANES))
            mass = jnp.minimum(mass, cols[h])
        return jnp.min(mass)

    def pv_pair(pr, att, start, n_keys):
        vblk = v_ref[0, pl.ds(start, n_keys), pr * LANES:(pr + 1) * LANES]
        lo = lax.broadcasted_iota(jnp.int32, (n_keys, LANES), 1) < HEAD_DIM
        v2 = jnp.concatenate([jnp.where(lo, vblk, jnp.zeros_like(vblk)),
                              jnp.where(lo, jnp.zeros_like(vblk), vblk)], axis=0)
        att2 = jnp.concatenate([att[2 * pr].astype(BF16), att[2 * pr + 1].astype(BF16)], axis=1)
        return _dot(att2, v2)

    def block(j):
        start = pl.multiple_of(j * tq, tq)
        tri = tri_ref[...]
        z = [_dot_nt(qm_sc[h], k_pair(h, start, tq)) for h in heads]
        sp = [_softplus2(z[h]) for h in heads]
        tail = [_dot(sp[h].astype(BF16), tri)
                + jnp.concatenate([carry_sc[h]] * (tq // LANES), axis=1) for h in heads]
        att = [jnp.exp2(z[h] - tail[h]) for h in heads]
        for pr in range(N_PAIRS):
            acc_sc[pr] += pv_pair(pr, att, start, tq)
        return set_mass([t_[:, 0:1] for t_ in tail])

    def first_blocks(with_prev):
        start = pl.multiple_of(i * tq, tq)
        prev = pl.multiple_of(jnp.maximum(i - 1, 0) * tq, tq)
        tri = tri_ref[...]
        keep = lambda u: jnp.where(causal, u, 0.0)
        right = lambda u: jnp.concatenate([u[:, :half], keep(u[:, half:])], axis=1)
        z_t = [_dot_nt(qm_sc[h, :half], k_pair(h, start, half)) for h in heads]
        z_b = [_dot_nt(qm_sc[h, half:], k_pair(h, start, tq)) for h in heads]
        z_p = [_dot_nt(qm_sc[h], k_pair(h, prev, tq)) for h in heads] if with_prev else []
        sp_t = [keep(_softplus2(z)) for z in z_t]
        sp_b = [right(_softplus2(z)) for z in z_b]
        sp_p = [_softplus2(z) for z in z_p]
        tail_t = [_dot(s.astype(BF16), tri[:half, :half]) for s in sp_t]
        tail_b = [_dot(s.astype(BF16), tri) for s in sp_b]
        tail_p = [_dot(s.astype(BF16), tri) for s in sp_p]
        att_t = [keep(jnp.exp2(z_t[h] - tail_t[h])) for h in heads]
        att_b = [right(jnp.exp2(z_b[h] - tail_b[h])) for h in heads]
        cols = [jnp.concatenate([tail_t[h][:, 0:1], tail_b[h][:, 0:1]], axis=0) for h in heads]
        if with_prev:
            tail_p = [tail_p[h] + cols[h] for h in heads]
            att_p = [jnp.exp2(z_p[h] - tail_p[h]) for h in heads]
            cols = [t_[:, 0:1] for t_ in tail_p]
        for pr in range(N_PAIRS):
            pv = jnp.concatenate([pv_pair(pr, att_t, start, half), pv_pair(pr, att_b, start, tq)], axis=0)
            acc_sc[pr] = pv + pv_pair(pr, att_p, prev, tq) if with_prev else pv
        return set_mass(cols)

    def later_tile():
        def body(state):
            jj, _ = state
            return jj + 1, block(i - 2 - jj)

        lax.while_loop(lambda s: jnp.logical_and(s[0] < i - 1, s[1] < SKIP_BITS), body,
                       (jnp.int32(0), first_blocks(True)))

    def tile_zero():
        first_blocks(False)

    if maybe_first:
        pl.when(i == 0)(tile_zero)
        pl.when(i > 0)(later_tile)
    else:
        later_tile()

    for pr in range(N_PAIRS):
        y_ref[0, rows, pr * LANES:(pr + 1) * LANES] = acc_sc[pr]


def _sb_prompt(qb, kb, vb, tri, tq):
    b, t, _ = qb.shape
    n_sub = PROMPT_TILES
    full = pl.BlockSpec((1, t, D_SB), lambda i, j: (i, 0, 0))
    tile = pl.BlockSpec((1, n_sub * tq, D_SB), lambda i, j: (i, j, 0))
    return pl.pallas_call(
        functools.partial(_sb_prompt_kernel, tq=tq, n_sub=n_sub),
        grid=(b, t // (n_sub * tq)),
        in_specs=[tile, full, full, pl.BlockSpec(tri.shape, lambda i, j: (0, 0))],
        out_specs=tile,
        out_shape=jax.ShapeDtypeStruct((b, t, D_SB), F32),
        scratch_shapes=[pltpu.VMEM((N_HEADS, tq, LANES), BF16),
                        pltpu.VMEM((N_PAIRS, tq, LANES), F32),
                        pltpu.VMEM((N_HEADS, tq, LANES), F32)],
        compiler_params=pltpu.CompilerParams(dimension_semantics=("parallel", "arbitrary"),
                                             vmem_limit_bytes=VMEM_LIMIT),
        name="sb_prompt",
    )(qb, kb, vb, tri)


def _sb_sample_kernel(q_ref, kn_ref, vn_ref, kct_ref, vct_ref, tri_ref, trin_ref, y_ref):
    t = q_ref.shape[1]
    past = kct_ref.shape[3]
    tri, trin = tri_ref[...], trin_ref[...]
    tk = tri.shape[0]
    nb = past // tk
    causal = (lax.broadcasted_iota(jnp.int32, (t, t), 1) < lax.broadcasted_iota(jnp.int32, (t, t), 0))
    lane_lo = lax.broadcasted_iota(jnp.int32, (t, LANES), 1) < HEAD_DIM
    row_lo = lax.broadcasted_iota(jnp.int32, (LANES, past), 0) < HEAD_DIM
    units = [(s, h) for s in range(q_ref.shape[0]) for h in range(N_HEADS)]
    pair = lambda ref, u: ref[u[0], :, (u[1] // 2) * LANES:(u[1] // 2 + 1) * LANES]
    own = lambda u, x, mask: jnp.where(mask if u[1] % 2 == 0 else jnp.logical_not(mask), x,
                                       jnp.zeros_like(x))
    cache = lambda ref, u: ref[u[0], 2 * (u[1] // 2):2 * (u[1] // 2) + 2].reshape(LANES, past).astype(BF16)
    qm = {u: own(u, pair(q_ref, u), lane_lo) for u in units}
    zn = {u: _dot_nt(qm[u], pair(kn_ref, u)) for u in units}
    zp = {u: _dot(qm[u], cache(kct_ref, u)) for u in units}
    spn = {u: jnp.where(causal, _softplus2(zn[u]), 0.0) for u in units}
    spp = {u: _softplus2(zp[u]) for u in units}
    tailn = {u: _dot(spn[u].astype(BF16), trin) for u in units}
    tailp = {u: [_dot(spp[u][:, c * tk:(c + 1) * tk].astype(BF16), tri) for c in range(nb)]
             for u in units}
    for u in units:
        off = jnp.broadcast_to(tailn[u][:, 0:1], (t, tk))
        for c in reversed(range(nb)):
            tailp[u][c] = tailp[u][c] + off
            off = jnp.broadcast_to(tailp[u][c][:, 0:1], (t, tk))
    out = {}
    for u in units:
        attn = jnp.where(causal, jnp.exp2(zn[u] - tailn[u]), 0.0).astype(BF16)
        attp = jnp.exp2(zp[u] - jnp.concatenate(tailp[u], axis=1)).astype(BF16)
        yh = (_dot(attn, own(u, pair(vn_ref, u), lane_lo))
              + _dot_nt(attp, own(u, cache(vct_ref, u), row_lo)))
        key = (u[0], u[1] // 2)
        out[key] = yh if key not in out else out[key] + yh
    for (s, pr), val in out.items():
        y_ref[s, :, pr * LANES:(pr + 1) * LANES] = val


def _sb_sample(qb, kb, vb, kct, vct, tri, trin):
    b, t, _ = qb.shape
    past = kct.shape[3]
    ns = SAMPLE_SEQS
    new = pl.BlockSpec((ns, t, D_SB), lambda i: (i, 0, 0))
    old = pl.BlockSpec((ns, N_HEADS, HEAD_DIM, past), lambda i: (i, 0, 0, 0))
    const = lambda a: pl.BlockSpec(a.shape, lambda i: (0, 0))
    return pl.pallas_call(
        _sb_sample_kernel,
        grid=(b // ns,),
        in_specs=[new, new, new, old, old, const(tri), const(trin)],
        out_specs=new,
        out_shape=jax.ShapeDtypeStruct((b, t, D_SB), F32),
        compiler_params=pltpu.CompilerParams(dimension_semantics=("parallel",),
                                             vmem_limit_bytes=VMEM_LIMIT),
        name="sb_sample",
    )(qb, kb, vb, kct, vct, tri, trin)


def _merge_ffn_kernel(x_ref, ya_ref, yb_ref, gate_ref, wua_ref, wub_ref, wo_ref, g2_ref,
                      w1_ref, w2_ref, y_ref, *, ff_chunk):
    gate = gate_ref[...]
    merged = (gate[:, 0:D_MODEL] * _dot(ya_ref[...].astype(BF16), wua_ref[...])
              + gate[:, D_MODEL:2 * D_MODEL] * _dot(yb_ref[...].astype(BF16), wub_ref[...]))
    x = x_ref[...] + _dot(merged.astype(BF16), wo_ref[...])
    ms = jnp.mean(x * x, axis=-1, keepdims=True)
    xn = (x * lax.rsqrt(ms + RMS_EPS) * g2_ref[...]).astype(BF16)
    for c in range(D_FF // ff_chunk):
        sl = slice(c * ff_chunk, (c + 1) * ff_chunk)
        h = jnp.maximum(_dot(xn, w1_ref[:, sl]), 0.0)
        x = x + _dot((h * h).astype(BF16), w2_ref[sl, :])
    y_ref[...] = x


def _merge_ffn(x2d, ya, yb, gate, wua, wub, wo, g2, w1, w2, tm, ff_chunk):
    n = x2d.shape[0]
    row = lambda w: pl.BlockSpec((tm, w), lambda i: (i, 0))
    const = lambda a: pl.BlockSpec(a.shape, lambda i: (0, 0), pipeline_mode=pl.Buffered(1))
    return pl.pallas_call(
        functools.partial(_merge_ffn_kernel, ff_chunk=ff_chunk),
        grid=(n // tm,),
        in_specs=[row(D_MODEL), row(D_RWKV), row(D_SB), row(2 * D_MODEL),
                  const(wua), const(wub), const(wo), const(g2), const(w1), const(w2)],
        out_specs=row(D_MODEL),
        out_shape=jax.ShapeDtypeStruct((n, D_MODEL), F32),
        compiler_params=pltpu.CompilerParams(dimension_semantics=("parallel",),
                                             vmem_limit_bytes=VMEM_LIMIT),
        name="merge_ffn",
    )(x2d, ya, yb, gate, wua, wub, wo, g2, w1, w2)


def _pad_cols(a, width):
    return jnp.pad(a, [(0, 0)] * (a.ndim - 1) + [(0, width - a.shape[-1])])


def _pad_rwkv_cols(a):
    o1 = 3 * D_RWKV + D_DECAY_LORA
    o2 = o1 + D_AAA_LORA
    return jnp.concatenate([a[..., :3 * D_RWKV],
                            _pad_cols(a[..., 3 * D_RWKV:o1], LANES),
                            _pad_cols(a[..., o1:o2], LANES),
                            _pad_cols(a[..., o2:], 2 * LANES)], axis=-1)


def _unpad_rwkv_cols(a):
    return jnp.concatenate([a[..., :PW_OFF + D_DECAY_LORA],
                            a[..., PA_OFF:PA_OFF + D_AAA_LORA],
                            a[..., PG_OFF:PG_OFF + D_GATE_LORA]], axis=-1)


def _state_to_pairs(s):
    b = s.shape[0]
    s = s.reshape(b, N_PAIRS, 2, HEAD_DIM, HEAD_DIM)
    z = jnp.zeros_like(s[:, :, 0])
    top = jnp.concatenate([s[:, :, 0], z], axis=-1)
    bot = jnp.concatenate([z, s[:, :, 1]], axis=-1)
    return jnp.concatenate([top, bot], axis=-2)


def _pairs_to_state(sp):
    b = sp.shape[0]
    h0 = sp[:, :, :HEAD_DIM, :HEAD_DIM]
    h1 = sp[:, :, HEAD_DIM:, HEAD_DIM:]
    return jnp.stack([h0, h1], axis=2).reshape(b, N_HEADS, HEAD_DIM, HEAD_DIM)


def _tri_ge(n):
    i = np.arange(n)
    return i[:, None] >= i[None, :]


def _layer(x, shift_prev, s0, kt_past, vt_past, wts, consts):
    bsz, t, _ = x.shape
    bb = min(bsz, RWKV_ROWS // CHUNK)
    tt = RWKV_ROWS // bb
    n = bsz * t
    x2d = x.reshape(n, D_MODEL)
    flat = t % DENSE_TILE != 0
    pr3, qb, kt, vt, kb, vb, gate = _inproj(x2d[None] if flat else x, wts["g1"], wts["w_all"],
                                            wts["gq"], wts["gk"], consts["bd"], DENSE_TILE)
    if flat:
        pr3 = pr3.reshape(bsz, t, D_RWKV_PAD)
        qb, kb, vb = (a.reshape(bsz, t, D_SB) for a in (qb, kb, vb))
        unflat = lambda a: a.reshape(N_HEADS, HEAD_DIM, bsz, t).transpose(2, 0, 3, 1)
        k_new, v_new = unflat(kt), unflat(vt)
    else:
        k_new, v_new = jnp.swapaxes(kt, 2, 3), jnp.swapaxes(vt, 2, 3)
    gate = gate.reshape(n, 2 * D_MODEL)
    shift_new = _unpad_rwkv_cols(pr3[:, t - 1:t, :])
    ya, s_pairs = _rwkv(pr3, _pad_rwkv_cols(shift_prev), _state_to_pairs(s0.astype(F32)),
                        wts["rwkv"], consts["bd"], consts["tri_chunks"], bb, tt)
    if kt_past is None:
        yb = _sb_prompt(qb, kb, vb, consts["tri"], TILE)
    else:
        yb = _sb_sample(qb, kb, vb, kt_past, vt_past, consts["tri"], consts["tri_new"])
    y = _merge_ffn(x2d, ya.reshape(n, D_RWKV), yb.reshape(n, D_SB), gate, wts["wua"], wts["wub"],
                   wts["wo"], wts["g2"], wts["w1"], wts["w2"], DENSE_TILE, FF_CHUNK)
    return y.reshape(bsz, t, D_MODEL), shift_new, _pairs_to_state(s_pairs), k_new, v_new


def _prep_weights(g_norm1, w_in, rwkv_mu, rwkv_w0, rwkv_w2, rwkv_a0, rwkv_a2, rwkv_g2, rwkv_k_k,
                  rwkv_k_a, rwkv_r_k, rwkv_lnx_w, rwkv_lnx_b, sb_q_norm_g, sb_k_norm_g, w_up_a,
                  w_up_b, w_out, g_norm2, w_ff1, w_ff2):
    row = lambda a: a.reshape(1, -1).astype(F32)
    w_all = jnp.concatenate([_pad_rwkv_cols(w_in[:, :D_RWKV_IN]), w_in[:, D_RWKV_IN:]], axis=1)
    pad_rows = lambda a, rows: jnp.pad(a, ((0, rows - a.shape[0]), (0, 0))).astype(BF16)
    rwkv = (row(_pad_rwkv_cols(rwkv_mu)), row(rwkv_w0), pad_rows(rwkv_w2, LANES), row(rwkv_a0),
            pad_rows(rwkv_a2, LANES), pad_rows(rwkv_g2, 2 * LANES), row(rwkv_k_k), row(rwkv_k_a),
            row(rwkv_r_k), row(rwkv_lnx_w), row(rwkv_lnx_b))
    return {
        "g1": row(g_norm1), "w_all": w_all.astype(BF16),
        "gq": row(jnp.tile(sb_q_norm_g, N_HEADS)), "gk": row(jnp.tile(sb_k_norm_g, N_HEADS)),
        "rwkv": rwkv,
        "wua": w_up_a.astype(BF16), "wub": w_up_b.astype(BF16), "wo": w_out.astype(BF16),
        "g2": row(g_norm2), "w1": w_ff1.astype(BF16), "w2": w_ff2.astype(BF16),
    }


def _constants(t_new):
    hd = np.arange(2 * LANES) // HEAD_DIM
    ch = np.arange(RWKV_ROWS) // CHUNK
    mats = {
        "bd": hd[:, None] == hd[None, :],
        "tri_chunks": _tri_ge(RWKV_ROWS) & (ch[:, None] == ch[None, :]),
        "tri": _tri_ge(TILE),
        "tri_new": _tri_ge(t_new),
    }
    return {name: jnp.asarray(m.astype(np.float32), BF16) for name, m in mats.items()}


def kernel(x_prompt, x_sample, cache_sb_k, cache_sb_v, state_rwkv_wkv, state_rwkv_shift, g_norm1, w_in, rwkv_mu, rwkv_w0, rwkv_w2, rwkv_a0, rwkv_a2, rwkv_g2, rwkv_k_k, rwkv_k_a, rwkv_r_k, rwkv_lnx_w, rwkv_lnx_b, sb_q_norm_g, sb_k_norm_g, w_up_a, w_up_b, w_out, g_norm2, w_ff1, w_ff2):
    depth = w_in.shape[0]
    assert depth == 1
    layer_w = (g_norm1, w_in, rwkv_mu, rwkv_w0, rwkv_w2, rwkv_a0, rwkv_a2, rwkv_g2, rwkv_k_k,
               rwkv_k_a, rwkv_r_k, rwkv_lnx_w, rwkv_lnx_b, sb_q_norm_g, sb_k_norm_g, w_up_a,
               w_up_b, w_out, g_norm2, w_ff1, w_ff2)
    wts = _prep_weights(*(a[0] for a in layer_w))
    consts = _constants(x_sample.shape[1])
    bp = x_prompt.shape[0]
    s0_p = jnp.zeros((bp, N_HEADS, HEAD_DIM, HEAD_DIM), F32)
    shift0_p = jnp.zeros((bp, 1, D_RWKV_IN), x_prompt.dtype)
    y_p, sh_p, s_p, k_p, v_p = _layer(x_prompt, shift0_p, s0_p, None, None, wts, consts)
    y_s, sh_s, s_s, k_s, v_s = _layer(x_sample, state_rwkv_shift[0], state_rwkv_wkv[0],
                                      jnp.swapaxes(cache_sb_k[0], 2, 3),
                                      jnp.swapaxes(cache_sb_v[0], 2, 3), wts, consts)
    return (y_p, y_s, k_p[None], v_p[None], s_p[None], sh_p[None],
            k_s[None], v_s[None], s_s[None], sh_s[None])
```

```python
import functools

import jax
import jax.numpy as jnp
import numpy as np
from jax import lax
from jax.experimental import pallas as pl
from jax.experimental.pallas import tpu as pltpu

F32 = jnp.float32
BF16 = jnp.bfloat16

D_MODEL = 1024
HEAD_DIM = 64
D_RWKV = 512
D_SB = 512
N_HEADS = 8
N_PAIRS = 4
LANES = 128
SUBLANES = 8
D_DECAY_LORA = 64
D_AAA_LORA = 64
D_GATE_LORA = 160
D_RWKV_IN = 3 * D_RWKV + D_DECAY_LORA + D_AAA_LORA + D_GATE_LORA
D_FF = 4 * D_MODEL
CHUNK = 64
RWKV_ROWS = 256
RWKV_TILES = 2
TILE = 256
PROMPT_TILES = 2
SAMPLE_SEQS = 2
DENSE_TILE = 512
FF_CHUNK = 1024
SB_SCALE = HEAD_DIM ** -0.5
LOG2E = 1.4426950408889634
Q_SCALE = SB_SCALE * LOG2E
SKIP_BITS = 150.0
RMS_EPS = 1e-6
GN_EPS = 64e-5
L2_EPS = 1e-24

PW_OFF = 3 * D_RWKV
PA_OFF = PW_OFF + LANES
PG_OFF = PA_OFF + LANES
D_RWKV_PAD = PG_OFF + 2 * LANES
SB_OFF = D_RWKV_PAD
GATE_OFF = SB_OFF + 3 * D_SB

VMEM_LIMIT = 56 * 1024 * 1024


def _dot(a, b):
    return jnp.dot(a, b, preferred_element_type=F32)


def _dot_nt(a, b):
    return lax.dot_general(a, b, (((1,), (1,)), ((), ())), preferred_element_type=F32)


def _split(x):
    hi = x.astype(BF16)
    lo = (x - hi.astype(F32)).astype(BF16)
    return hi, lo


def _dot_lh(m, x):
    hi, lo = _split(x)
    return _dot(m, hi) + _dot(m, lo)


def _head_sum(x, bd):
    xb = x.astype(BF16)
    w = bd.shape[0]
    return jnp.concatenate([_dot(xb[:, c * w:(c + 1) * w], bd) for c in range(x.shape[1] // w)],
                           axis=1)


def _mm(a, b):
    return _dot(a.astype(BF16), b.astype(BF16))


def _mm_nt(a, b):
    return _dot_nt(a.astype(BF16), b.astype(BF16))


def _softplus(z):
    return jnp.maximum(z, 0.0) + jnp.log(1.0 + jnp.exp(-jnp.abs(z)))


def _softplus2(z):
    return jnp.maximum(z, 0.0) + jnp.log2(1.0 + jnp.exp2(-jnp.abs(z)))


def _sigmoid(z):
    return 0.5 * jnp.tanh(0.5 * z) + 0.5


def _inproj_kernel(x_ref, g1_ref, w_ref, gq_ref, gk_ref, bd_ref,
                   pr_ref, q_ref, kh_ref, vh_ref, kb_ref, vb_ref, gate_ref):
    x = x_ref[0]
    ms = jnp.mean(x * x, axis=-1, keepdims=True)
    xn = (x * lax.rsqrt(ms + RMS_EPS) * g1_ref[...]).astype(BF16)
    pr_ref[0] = _dot(xn, w_ref[:, 0:D_RWKV_PAD])
    bd = bd_ref[...]

    def head_norm(u, g):
        ss = _head_sum(u * u, bd) * (1.0 / HEAD_DIM)
        return u * lax.rsqrt(ss + RMS_EPS) * g

    q = head_norm(_dot(xn, w_ref[:, SB_OFF:SB_OFF + D_SB]), gq_ref[...])
    k = head_norm(_dot(xn, w_ref[:, SB_OFF + D_SB:SB_OFF + 2 * D_SB]), gk_ref[...])
    v = _dot(xn, w_ref[:, SB_OFF + 2 * D_SB:SB_OFF + 3 * D_SB])
    q_ref[0] = (q * Q_SCALE).astype(BF16)
    kb_ref[0] = k.astype(BF16)
    vb_ref[0] = v.astype(BF16)
    kh_ref[0] = k.T.reshape(N_HEADS, HEAD_DIM, k.shape[0])
    vh_ref[0] = v.T.reshape(N_HEADS, HEAD_DIM, v.shape[0])
    gate_ref[0] = _sigmoid(_dot(xn, w_ref[:, GATE_OFF:GATE_OFF + 2 * D_MODEL]))


def _inproj(x, g1, w_all, gq, gk, bd, tm):
    b, t, _ = x.shape
    row = lambda w: pl.BlockSpec((1, tm, w), lambda i, j: (i, j, 0))
    heads = pl.BlockSpec((1, N_HEADS, HEAD_DIM, tm), lambda i, j: (i, 0, 0, j))
    const = lambda a: pl.BlockSpec(a.shape, lambda i, j: (0, 0), pipeline_mode=pl.Buffered(1))
    return pl.pallas_call(
        _inproj_kernel,
        grid=(b, t // tm),
        in_specs=[row(D_MODEL), const(g1), const(w_all), const(gq), const(gk), const(bd)],
        out_specs=[row(D_RWKV_PAD), row(D_SB), heads, heads, row(D_SB), row(D_SB),
                   row(2 * D_MODEL)],
        out_shape=[jax.ShapeDtypeStruct((b, t, D_RWKV_PAD), F32),
                   jax.ShapeDtypeStruct((b, t, D_SB), BF16),
                   jax.ShapeDtypeStruct((b, N_HEADS, HEAD_DIM, t), F32),
                   jax.ShapeDtypeStruct((b, N_HEADS, HEAD_DIM, t), F32),
                   jax.ShapeDtypeStruct((b, t, D_SB), BF16),
                   jax.ShapeDtypeStruct((b, t, D_SB), BF16),
                   jax.ShapeDtypeStruct((b, t, 2 * D_MODEL), F32)],
        compiler_params=pltpu.CompilerParams(dimension_semantics=("parallel", "parallel"),
                                             vmem_limit_bytes=VMEM_LIMIT),
        name="inproj",
    )(x, g1, w_all, gq, gk, bd)


def _rwkv_kernel(p_ref, shift_ref, s0_ref, mu_ref, w0_ref, w2_ref, a0_ref, a2_ref, g2_ref,
                 kk_ref, ka_ref, rk_ref, lw_ref, lb_ref, bd_ref, tri_ref,
                 y_ref, sout_ref, carry_sc, s_sc, *, bb, tt, n_sub):
    @pl.when(pl.program_id(1) == 0)
    def _():
        carry_sc[...] = shift_ref[...]
        s_sc[...] = s0_ref[...]

    refs = (mu_ref, w0_ref, w2_ref, a0_ref, a2_ref, g2_ref, kk_ref, ka_ref, rk_ref, lw_ref, lb_ref,
            bd_ref, tri_ref)
    for s in range(n_sub):
        _rwkv_tile(slice(s * tt, (s + 1) * tt), p_ref, refs, y_ref, sout_ref, carry_sc, s_sc, bb, tt)


def _rwkv_tile(tok, p_ref, refs, y_ref, sout_ref, carry_sc, s_sc, bb, tt):
    (mu_ref, w0_ref, w2_ref, a0_ref, a2_ref, g2_ref, kk_ref, ka_ref, rk_ref, lw_ref, lb_ref,
     bd_ref, tri_ref) = refs
    rows = bb * tt
    p = p_ref[:, tok, :].reshape(rows, D_RWKV_PAD)
    rolled = pltpu.roll(p, 1, 0)
    first_row = lax.broadcasted_iota(jnp.int32, (SUBLANES, D_RWKV_PAD), 0) == 0
    pieces = []
    for b_ in range(bb):
        lo = b_ * tt
        pieces += [jnp.where(first_row, carry_sc[b_], rolled[lo:lo + SUBLANES]),
                   rolled[lo + SUBLANES:lo + tt]]
        carry_sc[b_] = p[lo + tt - 1:lo + tt, :]
    prev = jnp.concatenate(pieces, axis=0)
    pm = p + (prev - p) * mu_ref[...]
    r = pm[:, 0:D_RWKV]
    k = pm[:, D_RWKV:2 * D_RWKV]
    v = pm[:, 2 * D_RWKV:3 * D_RWKV]
    wl = pm[:, PW_OFF:PW_OFF + LANES]
    al = pm[:, PA_OFF:PA_OFF + LANES]
    gl = pm[:, PG_OFF:PG_OFF + 2 * LANES]

    bd = bd_ref[...]
    w = -_softplus(-(w0_ref[...] + _mm(jnp.tanh(wl), w2_ref[...]))) - 0.5
    ld = -jnp.exp(w)
    a = _sigmoid(a0_ref[...] + _mm(al, a2_ref[...]))
    g = _mm(_sigmoid(gl), g2_ref[...])
    kk = k * kk_ref[...]
    kk = kk * lax.rsqrt(jnp.maximum(_head_sum(kk * kk, bd), L2_EPS))
    k = k * (1.0 + (a - 1.0) * ka_ref[...])
    bonus = _head_sum(r * k * rk_ref[...], bd) * v

    n_ch = tt // CHUNK
    chunks = [(b_, ch) for b_ in range(bb) for ch in range(n_ch)]
    rows_of = lambda b_, ch: slice(b_ * tt + ch * CHUNK, b_ * tt + (ch + 1) * CHUNK)
    cl = _dot_lh(tri_ref[...], ld)
    cl_end = jnp.concatenate(
        [jnp.broadcast_to(cl[rows_of(*c).stop - 1:rows_of(*c).stop, :], (CHUNK, D_RWKV))
         for c in chunks], axis=0)
    w_inv = jnp.exp(-cl)
    w_end = jnp.exp(cl_end - cl)
    at = -kk * jnp.exp(cl - ld)
    bt = kk * a * w_inv
    kt = k * w_inv
    rt = r * jnp.exp(cl)
    be = kk * a * w_end
    ke = k * w_end
    wc = jnp.exp(cl_end)

    lane = lax.broadcasted_iota(jnp.int32, (CHUNK, LANES), 1)
    m0 = lane < HEAD_DIM
    ri = lax.broadcasted_iota(jnp.int32, (LANES, LANES), 0)
    ci = lax.broadcasted_iota(jnp.int32, (LANES, LANES), 1)
    same = (ri // CHUNK) == (ci // CHUNK)
    strict = same & ((ri % CHUNK) > (ci % CHUNK))
    incl = same & ((ri % CHUNK) >= (ci % CHUNK))
    eye = (ri == ci).astype(F32)

    def stack(u):
        zero = jnp.zeros_like(u)
        return jnp.concatenate([jnp.where(m0, u, zero), jnp.where(m0, zero, u)], axis=0)

    stack_b = lambda u: stack(u.astype(BF16))

    keys = [(b_, ch, pr) for (b_, ch) in chunks for pr in range(N_PAIRS)]
    blk = lambda x, key: x[rows_of(key[0], key[1]), key[2] * LANES:(key[2] + 1) * LANES]
    la = {q: stack_b(blk(at, q)) for q in keys}
    lr = {q: stack(blk(rt, q)) for q in keys}
    vst = {q: stack(blk(v, q)) for q in keys}
    twice = lambda u: jnp.concatenate([u, u], axis=0)
    g4 = {q: _mm_nt(jnp.concatenate([la[q], lr[q].astype(BF16)], axis=0),
                    jnp.concatenate([twice(blk(bt, q)), twice(blk(kt, q))], axis=0)) for q in keys}
    a_ab = {q: jnp.where(strict, g4[q][:LANES, :LANES], 0.0) for q in keys}
    a_ak = {q: jnp.where(strict, g4[q][:LANES, LANES:], 0.0) for q in keys}
    a_rb = {q: jnp.where(incl, g4[q][LANES:, :LANES], 0.0) for q in keys}
    a_rk = {q: jnp.where(incl, g4[q][LANES:, LANES:], 0.0) for q in keys}
    tinv = {q: eye + a_ab[q] for q in keys}
    apow = {q: _mm(a_ab[q], a_ab[q]) for q in keys}
    for _ in range(4):
        nxt = {q: _mm(apow[q], jnp.concatenate([apow[q], tinv[q]], axis=1)) for q in keys}
        apow = {q: nxt[q][:, :LANES] for q in keys}
        tinv = {q: tinv[q] + nxt[q][:, LANES:] for q in keys}
    tinv = {q: tinv[q] + _mm(apow[q], tinv[q]) for q in keys}
    akv = {q: _mm(jnp.concatenate([a_ak[q], a_rk[q]], axis=0), vst[q]) for q in keys}
    tlav = {q: _mm(tinv[q], jnp.concatenate([la[q], akv[q][:LANES].astype(BF16)], axis=1))
            for q in keys}
    rlon = {q: _mm(a_rb[q], tlav[q]) for q in keys}
    rl = {q: lr[q] + rlon[q][:, :LANES] for q in keys}
    on = {q: rlon[q][:, LANES:] + akv[q][LANES:] for q in keys}
    mn = {q: _mm(tlav[q].T, stack_b(blk(be, q))) for q in keys}
    nc = {q: mn[q][LANES:] + _mm(vst[q].T, stack_b(blk(ke, q))) for q in keys}

    out_rows = []
    for b_ in range(bb):
        state = [s_sc[b_, pr] for pr in range(N_PAIRS)]
        for ch in range(n_ch):
            outs = []
            for pr in range(N_PAIRS):
                q = (b_, ch, pr)
                s_b = state[pr].astype(BF16)
                o_st = _dot_nt(rl[q].astype(BF16), s_b) + on[q]
                outs.append(o_st[0:CHUNK] + o_st[CHUNK:2 * CHUNK])
                state[pr] = (state[pr] * blk(wc, q)[0:1, :] + _dot(s_b, mn[q][:LANES].astype(BF16))
                             + nc[q])
            out_rows.append(jnp.concatenate(outs, axis=1))
        for pr in range(N_PAIRS):
            s_sc[b_, pr] = state[pr]
            sout_ref[b_, pr] = state[pr]
    o = jnp.concatenate(out_rows, axis=0)

    mean = _head_sum(o, bd) * (1.0 / HEAD_DIM)
    oc = o - mean
    var = _head_sum(oc * oc, bd) * (1.0 / HEAD_DIM)
    o = oc * lax.rsqrt(var + GN_EPS)
    y_ref[:, tok, :] = ((o * lw_ref[...] + lb_ref[...] + bonus) * g).reshape(bb, tt, D_RWKV)


def _rwkv(p3d, shift_pad, s0_bd, params, bd, tri, bb, tt):
    b, t, _ = p3d.shape
    n_sub = min(RWKV_TILES, t // tt)
    const = lambda a: pl.BlockSpec(a.shape, lambda i, j: (0,) * a.ndim)
    state = pl.BlockSpec((bb, N_PAIRS, LANES, LANES), lambda i, j: (i, 0, 0, 0))
    return pl.pallas_call(
        functools.partial(_rwkv_kernel, bb=bb, tt=tt, n_sub=n_sub),
        grid=(b // bb, t // (n_sub * tt)),
        in_specs=[pl.BlockSpec((bb, n_sub * tt, D_RWKV_PAD), lambda i, j: (i, j, 0)),
                  pl.BlockSpec((bb, 1, D_RWKV_PAD), lambda i, j: (i, 0, 0)), state]
                 + [const(a) for a in params] + [const(bd), const(tri)],
        out_specs=[pl.BlockSpec((bb, n_sub * tt, D_RWKV), lambda i, j: (i, j, 0)), state],
        out_shape=[jax.ShapeDtypeStruct((b, t, D_RWKV), F32),
                   jax.ShapeDtypeStruct((b, N_PAIRS, LANES, LANES), F32)],
        scratch_shapes=[pltpu.VMEM((bb, 1, D_RWKV_PAD), F32),
                        pltpu.VMEM((bb, N_PAIRS, LANES, LANES), F32)],
        compiler_params=pltpu.CompilerParams(dimension_semantics=("parallel", "arbitrary"),
                                             vmem_limit_bytes=VMEM_LIMIT),
        name="rwkv",
    )(p3d, shift_pad, s0_bd, *params, bd, tri)


def _sb_prompt_kernel(q_ref, k_ref, v_ref, tri_ref, y_ref, qm_sc, acc_sc, carry_sc, *, tq, n_sub):
    for s in range(n_sub):
        _sb_prompt_tile(pl.program_id(1) * n_sub + s, slice(s * tq, (s + 1) * tq), s == 0,
                        q_ref, k_ref, v_ref, tri_ref, y_ref, qm_sc, acc_sc, carry_sc, tq)


def _sb_prompt_tile(i, rows, maybe_first, q_ref, k_ref, v_ref, tri_ref, y_ref, qm_sc, acc_sc,
                    carry_sc, tq):
    half = tq // 2
    lane_q = lax.broadcasted_iota(jnp.int32, (tq, LANES), 1) < HEAD_DIM
    causal = (lax.broadcasted_iota(jnp.int32, (half, half), 1)
              < lax.broadcasted_iota(jnp.int32, (half, half), 0))
    for pr in range(N_PAIRS):
        qp = q_ref[0, rows, pr * LANES:(pr + 1) * LANES]
        qm_sc[2 * pr] = jnp.where(lane_q, qp, jnp.zeros_like(qp))
        qm_sc[2 * pr + 1] = jnp.where(lane_q, jnp.zeros_like(qp), qp)

    heads = range(N_HEADS)
    k_pair = lambda h, start, n: k_ref[0, pl.ds(start, n), (h // 2) * LANES:(h // 2 + 1) * LANES]

    def set_mass(cols):
        mass = cols[0]
        for h in heads:
            carry_sc[h] = jnp.broadcast_to(cols[h], (tq, LANES))
            mass = jnp.minimum(mass, cols[h])
        return jnp.min(mass)

    def pv_pair(pr, att, start, n_keys):
        vblk = v_ref[0, pl.ds(start, n_keys), pr * LANES:(pr + 1) * LANES]
        lo = lax.broadcasted_iota(jnp.int32, (n_keys, LANES), 1) < HEAD_DIM
        v2 = jnp.concatenate([jnp.where(lo, vblk, jnp.zeros_like(vblk)),
                              jnp.where(lo, jnp.zeros_like(vblk), vblk)], axis=0)
        att2 = jnp.concatenate([att[2 * pr].astype(BF16), att[2 * pr + 1].astype(BF16)], axis=1)
        return _dot(att2, v2)

    def block(j):
        start = pl.multiple_of(j * tq, tq)
        tri = tri_ref[...]
        z = [_dot_nt(qm_sc[h], k_pair(h, start, tq)) for h in heads]
        sp = [_softplus2(z[h]) for h in heads]
        tail = [_dot(sp[h].astype(BF16), tri)
                + jnp.concatenate([carry_sc[h]] * (tq // LANES), axis=1) for h in heads]
        att = [jnp.exp2(z[h] - tail[h]) for h in heads]
        for pr in range(N_PAIRS):
            acc_sc[pr] += pv_pair(pr, att, start, tq)
        return set_mass([t_[:, 0:1] for t_ in tail])

    def first_blocks(with_prev):
        start = pl.multiple_of(i * tq, tq)
        prev = pl.multiple_of(jnp.maximum(i - 1, 0) * tq, tq)
        tri = tri_ref[...]
        keep = lambda u: jnp.where(causal, u, 0.0)
        right = lambda u: jnp.concatenate([u[:, :half], keep(u[:, half:])], axis=1)
        z_t = [_dot_nt(qm_sc[h, :half], k_pair(h, start, half)) for h in heads]
        z_b = [_dot_nt(qm_sc[h, half:], k_pair(h, start, tq)) for h in heads]
        z_p = [_dot_nt(qm_sc[h], k_pair(h, prev, tq)) for h in heads] if with_prev else []
        sp_t = [keep(_softplus2(z)) for z in z_t]
        sp_b = [right(_softplus2(z)) for z in z_b]
        sp_p = [_softplus2(z) for z in z_p]
        tail_t = [_dot(s.astype(BF16), tri[:half, :half]) for s in sp_t]
        tail_b = [_dot(s.astype(BF16), tri) for s in sp_b]
        tail_p = [_dot(s.astype(BF16), tri) for s in sp_p]
        att_t = [keep(jnp.exp2(z_t[h] - tail_t[h])) for h in heads]
        att_b = [right(jnp.exp2(z_b[h] - tail_b[h])) for h in heads]
        cols = [jnp.concatenate([tail_t[h][:, 0:1], tail_b[h][:, 0:1]], axis=0) for h in heads]
        if with_prev:
            tail_p = [tail_p[h] + cols[h] for h in heads]
            att_p = [jnp.exp2(z_p[h] - tail_p[h]) for h in heads]
            cols = [t_[:, 0:1] for t_ in tail_p]
        for pr in range(N_PAIRS):
            pv = jnp.concatenate([pv_pair(pr, att_t, start, half), pv_pair(pr, att_b, start, tq)], axis=0)
            acc_sc[pr] = pv + pv_pair(pr, att_p, prev, tq) if with_prev else pv
        return set_mass(cols)

    def later_tile():
        def body(state):
            jj, _ = state
            return jj + 1, block(i - 2 - jj)

        lax.while_loop(lambda s: jnp.logical_and(s[0] < i - 1, s[1] < SKIP_BITS), body,
                       (jnp.int32(0), first_blocks(True)))

    def tile_zero():
        first_blocks(False)

    if maybe_first:
        pl.when(i == 0)(tile_zero)
        pl.when(i > 0)(later_tile)
    else:
        later_tile()

    for pr in range(N_PAIRS):
        y_ref[0, rows, pr * LANES:(pr + 1) * LANES] = acc_sc[pr]


def _sb_prompt(qb, kb, vb, tri, tq):
    b, t, _ = qb.shape
    n_sub = PROMPT_TILES
    full = pl.BlockSpec((1, t, D_SB), lambda i, j: (i, 0, 0))
    tile = pl.BlockSpec((1, n_sub * tq, D_SB), lambda i, j: (i, j, 0))
    return pl.pallas_call(
        functools.partial(_sb_prompt_kernel, tq=tq, n_sub=n_sub),
        grid=(b, t // (n_sub * tq)),
        in_specs=[tile, full, full, pl.BlockSpec(tri.shape, lambda i, j: (0, 0))],
        out_specs=tile,
        out_shape=jax.ShapeDtypeStruct((b, t, D_SB), F32),
        scratch_shapes=[pltpu.VMEM((N_HEADS, tq, LANES), BF16),
                        pltpu.VMEM((N_PAIRS, tq, LANES), F32),
                        pltpu.VMEM((N_HEADS, tq, LANES), F32)],
        compiler_params=pltpu.CompilerParams(dimension_semantics=("parallel", "arbitrary"),
                                             vmem_limit_bytes=VMEM_LIMIT),
        name="sb_prompt",
    )(qb, kb, vb, tri)


def _sb_sample_kernel(q_ref, kn_ref, vn_ref, kct_ref, vct_ref, tri_ref, trin_ref, y_ref):
    t = q_ref.shape[1]
    past = kct_ref.shape[3]
    tri, trin = tri_ref[...], trin_ref[...]
    tk = tri.shape[0]
    nb = past // tk
    causal = (lax.broadcasted_iota(jnp.int32, (t, t), 1) < lax.broadcasted_iota(jnp.int32, (t, t), 0))
    lane_lo = lax.broadcasted_iota(jnp.int32, (t, LANES), 1) < HEAD_DIM
    row_lo = lax.broadcasted_iota(jnp.int32, (LANES, past), 0) < HEAD_DIM
    units = [(s, h) for s in range(q_ref.shape[0]) for h in range(N_HEADS)]
    pair = lambda ref, u: ref[u[0], :, (u[1] // 2) * LANES:(u[1] // 2 + 1) * LANES]
    own = lambda u, x, mask: jnp.where(mask if u[1] % 2 == 0 else jnp.logical_not(mask), x,
                                       jnp.zeros_like(x))
    cache = lambda ref, u: ref[u[0], 2 * (u[1] // 2):2 * (u[1] // 2) + 2].reshape(LANES, past).astype(BF16)
    qm = {u: own(u, pair(q_ref, u), lane_lo) for u in units}
    zn = {u: _dot_nt(qm[u], pair(kn_ref, u)) for u in units}
    zp = {u: _dot(qm[u], cache(kct_ref, u)) for u in units}
    spn = {u: jnp.where(causal, _softplus2(zn[u]), 0.0) for u in units}
    spp = {u: _softplus2(zp[u]) for u in units}
    tailn = {u: _dot(spn[u].astype(BF16), trin) for u in units}
    tailp = {u: [_dot(spp[u][:, c * tk:(c + 1) * tk].astype(BF16), tri) for c in range(nb)]
             for u in units}
    for u in units:
        off = jnp.broadcast_to(tailn[u][:, 0:1], (t, tk))
        for c in reversed(range(nb)):
            tailp[u][c] = tailp[u][c] + off
            off = jnp.broadcast_to(tailp[u][c][:, 0:1], (t, tk))
    out = {}
    for u in units:
        attn = jnp.where(causal, jnp.exp2(zn[u] - tailn[u]), 0.0).astype(BF16)
        attp = jnp.exp2(zp[u] - jnp.concatenate(tailp[u], axis=1)).astype(BF16)
        yh = (_dot(attn, own(u, pair(vn_ref, u), lane_lo))
              + _dot_nt(attp, own(u, cache(vct_ref, u), row_lo)))
        key = (u[0], u[1] // 2)
        out[key] = yh if key not in out else out[key] + yh
    for (s, pr), val in out.items():
        y_ref[s, :, pr * LANES:(pr + 1) * LANES] = val


def _sb_sample(qb, kb, vb, kct, vct, tri, trin):
    b, t, _ = qb.shape
    past = kct.shape[3]
    ns = SAMPLE_SEQS
    new = pl.BlockSpec((ns, t, D_SB), lambda i: (i, 0, 0))
    old = pl.BlockSpec((ns, N_HEADS, HEAD_DIM, past), lambda i: (i, 0, 0, 0))
    const = lambda a: pl.BlockSpec(a.shape, lambda i: (0, 0))
    return pl.pallas_call(
        _sb_sample_kernel,
        grid=(b // ns,),
        in_specs=[new, new, new, old, old, const(tri), const(trin)],
        out_specs=new,
        out_shape=jax.ShapeDtypeStruct((b, t, D_SB), F32),
        compiler_params=pltpu.CompilerParams(dimension_semantics=("parallel",),
                                             vmem_limit_bytes=VMEM_LIMIT),
        name="sb_sample",
    )(qb, kb, vb, kct, vct, tri, trin)


def _merge_ffn_kernel(x_ref, ya_ref, yb_ref, gate_ref, wua_ref, wub_ref, wo_ref, g2_ref,
                      w1_ref, w2_ref, y_ref, *, ff_chunk):
    gate = gate_ref[...]
    merged = (gate[:, 0:D_MODEL] * _dot(ya_ref[...].astype(BF16), wua_ref[...])
              + gate[:, D_MODEL:2 * D_MODEL] * _dot(yb_ref[...].astype(BF16), wub_ref[...]))
    x = x_ref[...] + _dot(merged.astype(BF16), wo_ref[...])
    ms = jnp.mean(x * x, axis=-1, keepdims=True)
    xn = (x * lax.rsqrt(ms + RMS_EPS) * g2_ref[...]).astype(BF16)
    for c in range(D_FF // ff_chunk):
        sl = slice(c * ff_chunk, (c + 1) * ff_chunk)
        h = jnp.maximum(_dot(xn, w1_ref[:, sl]), 0.0)
        x = x + _dot((h * h).astype(BF16), w2_ref[sl, :])
    y_ref[...] = x


def _merge_ffn(x2d, ya, yb, gate, wua, wub, wo, g2, w1, w2, tm, ff_chunk):
    n = x2d.shape[0]
    row = lambda w: pl.BlockSpec((tm, w), lambda i: (i, 0))
    const = lambda a: pl.BlockSpec(a.shape, lambda i: (0, 0), pipeline_mode=pl.Buffered(1))
    return pl.pallas_call(
        functools.partial(_merge_ffn_kernel, ff_chunk=ff_chunk),
        grid=(n // tm,),
        in_specs=[row(D_MODEL), row(D_RWKV), row(D_SB), row(2 * D_MODEL),
                  const(wua), const(wub), const(wo), const(g2), const(w1), const(w2)],
        out_specs=row(D_MODEL),
        out_shape=jax.ShapeDtypeStruct((n, D_MODEL), F32),
        compiler_params=pltpu.CompilerParams(dimension_semantics=("parallel",),
                                             vmem_limit_bytes=VMEM_LIMIT),
        name="merge_ffn",
    )(x2d, ya, yb, gate, wua, wub, wo, g2, w1, w2)


def _pad_cols(a, width):
    return jnp.pad(a, [(0, 0)] * (a.ndim - 1) + [(0, width - a.shape[-1])])


def _pad_rwkv_cols(a):
    o1 = 3 * D_RWKV + D_DECAY_LORA
    o2 = o1 + D_AAA_LORA
    return jnp.concatenate([a[..., :3 * D_RWKV],
                            _pad_cols(a[..., 3 * D_RWKV:o1], LANES),
                            _pad_cols(a[..., o1:o2], LANES),
                            _pad_cols(a[..., o2:], 2 * LANES)], axis=-1)


def _unpad_rwkv_cols(a):
    return jnp.concatenate([a[..., :PW_OFF + D_DECAY_LORA],
                            a[..., PA_OFF:PA_OFF + D_AAA_LORA],
                            a[..., PG_OFF:PG_OFF + D_GATE_LORA]], axis=-1)


def _state_to_pairs(s):
    b = s.shape[0]
    s = s.reshape(b, N_PAIRS, 2, HEAD_DIM, HEAD_DIM)
    z = jnp.zeros_like(s[:, :, 0])
    top = jnp.concatenate([s[:, :, 0], z], axis=-1)
    bot = jnp.concatenate([z, s[:, :, 1]], axis=-1)
    return jnp.concatenate([top, bot], axis=-2)


def _pairs_to_state(sp):
    b = sp.shape[0]
    h0 = sp[:, :, :HEAD_DIM, :HEAD_DIM]
    h1 = sp[:, :, HEAD_DIM:, HEAD_DIM:]
    return jnp.stack([h0, h1], axis=2).reshape(b, N_HEADS, HEAD_DIM, HEAD_DIM)


def _tri_ge(n):
    i = np.arange(n)
    return i[:, None] >= i[None, :]


def _layer(x, shift_prev, s0, kt_past, vt_past, wts, consts):
    bsz, t, _ = x.shape
    bb = min(bsz, RWKV_ROWS // CHUNK)
    tt = RWKV_ROWS // bb
    n = bsz * t
    x2d = x.reshape(n, D_MODEL)
    flat = t % DENSE_TILE != 0
    pr3, qb, kt, vt, kb, vb, gate = _inproj(x2d[None] if flat else x, wts["g1"], wts["w_all"],
                                            wts["gq"], wts["gk"], consts["bd"], DENSE_TILE)
    if flat:
        pr3 = pr3.reshape(bsz, t, D_RWKV_PAD)
        qb, kb, vb = (a.reshape(bsz, t, D_SB) for a in (qb, kb, vb))
        unflat = lambda a: a.reshape(N_HEADS, HEAD_DIM, bsz, t).transpose(2, 0, 3, 1)
        k_new, v_new = unflat(kt), unflat(vt)
    else:
        k_new, v_new = jnp.swapaxes(kt, 2, 3), jnp.swapaxes(vt, 2, 3)
    gate = gate.reshape(n, 2 * D_MODEL)
    shift_new = _unpad_rwkv_cols(pr3[:, t - 1:t, :])
    ya, s_pairs = _rwkv(pr3, _pad_rwkv_cols(shift_prev), _state_to_pairs(s0.astype(F32)),
                        wts["rwkv"], consts["bd"], consts["tri_chunks"], bb, tt)
    if kt_past is None:
        yb = _sb_prompt(qb, kb, vb, consts["tri"], TILE)
    else:
        yb = _sb_sample(qb, kb, vb, kt_past, vt_past, consts["tri"], consts["tri_new"])
    y = _merge_ffn(x2d, ya.reshape(n, D_RWKV), yb.reshape(n, D_SB), gate, wts["wua"], wts["wub"],
                   wts["wo"], wts["g2"], wts["w1"], wts["w2"], DENSE_TILE, FF_CHUNK)
    return y.reshape(bsz, t, D_MODEL), shift_new, _pairs_to_state(s_pairs), k_new, v_new


def _prep_weights(g_norm1, w_in, rwkv_mu, rwkv_w0, rwkv_w2, rwkv_a0, rwkv_a2, rwkv_g2, rwkv_k_k,
                  rwkv_k_a, rwkv_r_k, rwkv_lnx_w, rwkv_lnx_b, sb_q_norm_g, sb_k_norm_g, w_up_a,
                  w_up_b, w_out, g_norm2, w_ff1, w_ff2):
    row = lambda a: a.reshape(1, -1).astype(F32)
    w_all = jnp.concatenate([_pad_rwkv_cols(w_in[:, :D_RWKV_IN]), w_in[:, D_RWKV_IN:]], axis=1)
    pad_rows = lambda a, rows: jnp.pad(a, ((0, rows - a.shape[0]), (0, 0))).astype(BF16)
    rwkv = (row(_pad_rwkv_cols(rwkv_mu)), row(rwkv_w0), pad_rows(rwkv_w2, LANES), row(rwkv_a0),
            pad_rows(rwkv_a2, LANES), pad_rows(rwkv_g2, 2 * LANES), row(rwkv_k_k), row(rwkv_k_a),
            row(rwkv_r_k), row(rwkv_lnx_w), row(rwkv_lnx_b))
    return {
        "g1": row(g_norm1), "w_all": w_all.astype(BF16),
        "gq": row(jnp.tile(sb_q_norm_g, N_HEADS)), "gk": row(jnp.tile(sb_k_norm_g, N_HEADS)),
        "rwkv": rwkv,
        "wua": w_up_a.astype(BF16), "wub": w_up_b.astype(BF16), "wo": w_out.astype(BF16),
        "g2": row(g_norm2), "w1": w_ff1.astype(BF16), "w2": w_ff2.astype(BF16),
    }


def _constants(t_new):
    hd = np.arange(2 * LANES) // HEAD_DIM
    ch = np.arange(RWKV_ROWS) // CHUNK
    mats = {
        "bd": hd[:, None] == hd[None, :],
        "tri_chunks": _tri_ge(RWKV_ROWS) & (ch[:, None] == ch[None, :]),
        "tri": _tri_ge(TILE),
        "tri_new": _tri_ge(t_new),
    }
    return {name: jnp.asarray(m.astype(np.float32), BF16) for name, m in mats.items()}


def kernel(x_prompt, x_sample, cache_sb_k, cache_sb_v, state_rwkv_wkv, state_rwkv_shift, g_norm1, w_in, rwkv_mu, rwkv_w0, rwkv_w2, rwkv_a0, rwkv_a2, rwkv_g2, rwkv_k_k, rwkv_k_a, rwkv_r_k, rwkv_lnx_w, rwkv_lnx_b, sb_q_norm_g, sb_k_norm_g, w_up_a, w_up_b, w_out, g_norm2, w_ff1, w_ff2):
    depth = w_in.shape[0]
    assert depth == 1
    layer_w = (g_norm1, w_in, rwkv_mu, rwkv_w0, rwkv_w2, rwkv_a0, rwkv_a2, rwkv_g2, rwkv_k_k,
               rwkv_k_a, rwkv_r_k, rwkv_lnx_w, rwkv_lnx_b, sb_q_norm_g, sb_k_norm_g, w_up_a,
               w_up_b, w_out, g_norm2, w_ff1, w_ff2)
    wts = _prep_weights(*(a[0] for a in layer_w))
    consts = _constants(x_sample.shape[1])
    bp = x_prompt.shape[0]
    s0_p = jnp.zeros((bp, N_HEADS, HEAD_DIM, HEAD_DIM), F32)
    shift0_p = jnp.zeros((bp, 1, D_RWKV_IN), x_prompt.dtype)
    y_p, sh_p, s_p, k_p, v_p = _layer(x_prompt, shift0_p, s0_p, None, None, wts, consts)
    y_s, sh_s, s_s, k_s, v_s = _layer(x_sample, state_rwkv_shift[0], state_rwkv_wkv[0],
                                      jnp.swapaxes(cache_sb_k[0], 2, 3),
                                      jnp.swapaxes(cache_sb_v[0], 2, 3), wts, consts)
    return (y_p, y_s, k_p[None], v_p[None], s_p[None], sh_p[None],
            k_s[None], v_s[None], s_s[None], sh_s[None])
```
